```python
import jax, jax.numpy as jnp
from jax import lax
import numpy as np

D_MODEL = 4096
BATCH = 4
SEQ = 2048
DEPTH = 2
DEC_BATCH = 8
DEC_SEQ = 1
PAST_LEN = 16384
PAGE_SIZE = 128

N_EVEN = (DEPTH + 1) // 2
N_ODD = DEPTH // 2
A_WIDTH = D_MODEL // 2
A_GROUPS = 16
A_CONV_WIDTH = 31
B_WIDTH = D_MODEL // 2
B_GROUPS = 16
B_CHUNK = 128
C_HEADS = 32
C_HEAD_DIM = D_MODEL // C_HEADS
C_WIDTH = C_HEADS * C_HEAD_DIM
C_PATTERNS = ((128, 1), (512, 4), (2048, 16))
C_MAX_WINDOW = 2048
C_BLOCK = 128
FFN_HIDDEN = ((8 * D_MODEL + 3 * 256 - 1) // (3 * 256)) * 256
RMS_EPS = 1e-6
LN_EPS = 1e-5

kernel_name = 'hybrid_conv_gmlp_dilated_attn_step'


def rms_norm(x, g):
    xf = x.astype(jnp.float32)
    y = xf * lax.rsqrt(jnp.mean(xf * xf, axis=-1, keepdims=True) + RMS_EPS)
    return (y * g.astype(jnp.float32)).astype(x.dtype)


def layer_norm_groups(x, g, b, groups):
    shp = x.shape
    xf = x.astype(jnp.float32).reshape(shp[:-1] + (groups, shp[-1] // groups))
    xc = xf - jnp.mean(xf, axis=-1, keepdims=True)
    y = (xc * lax.rsqrt(jnp.mean(xc * xc, axis=-1, keepdims=True) + LN_EPS)).reshape(shp)
    return (y * g.astype(jnp.float32) + b.astype(jnp.float32)).astype(x.dtype)


def causal_depthwise_conv(u_ext, w, b):
    c = u_ext.shape[-1]
    y = lax.conv_general_dilated(u_ext, w[:, None, :].astype(u_ext.dtype), window_strides=(1,),
                                 padding='VALID', dimension_numbers=('NWC', 'WIO', 'NWC'),
                                 feature_group_count=c)
    return y + b


def chunk_spatial_gate(v, w_s, b_s):
    n, t, _ = v.shape
    cg = B_WIDTH // B_GROUPS
    causal = jnp.tril(jnp.ones((B_CHUNK, B_CHUNK), dtype=bool))
    w_m = jnp.where(causal[None], w_s, 0)
    vc = v.reshape(n, t // B_CHUNK, B_CHUNK, B_GROUPS, cg)
    s = jnp.einsum('gij,nqjgc->nqigc', w_m, vc) + b_s.T[None, None, :, :, None]
    return s.reshape(n, t, B_WIDTH)


def conv_gating_mix(h, conv_hist, w_in, a_conv_w, a_conv_b, a_norm_g, a_norm_b,
                    b_norm_g, b_norm_b, b_spatial_w, b_spatial_b, w_out):
    n, t, _ = h.shape
    z = h @ w_in
    a_val, a_gate, b_u, b_v = jnp.split(z, [A_WIDTH, 2 * A_WIDTH, 2 * A_WIDTH + B_WIDTH], axis=-1)
    a_in = a_val * jax.nn.sigmoid(a_gate)
    a_ext = jnp.concatenate([conv_hist, a_in], axis=1)
    a = causal_depthwise_conv(a_ext, a_conv_w, a_conv_b)
    a = jax.nn.silu(layer_norm_groups(a, a_norm_g, a_norm_b, A_GROUPS))
    u = jax.nn.gelu(b_u, approximate=False)
    v = layer_norm_groups(jax.nn.gelu(b_v, approximate=False), b_norm_g, b_norm_b, 1)
    t_pad = -(-t // B_CHUNK) * B_CHUNK
    s = chunk_spatial_gate(jnp.pad(v, ((0, 0), (0, t_pad - t), (0, 0))), b_spatial_w, b_spatial_b)
    b = u * s[:, :t]
    y = jnp.concatenate([a, b], axis=-1) @ w_out
    return y, a_ext[:, -(A_CONV_WIDTH - 1):], v


def dilated_attention_prompt(q, k, v, window, dilation):
    n, s_len, h, e = q.shape
    reach = window // dilation
    m_len = s_len // dilation
    nblk = -(-m_len // C_BLOCK)
    m_pad = nblk * C_BLOCK

    def to_sub(t):
        t = t.reshape(n, m_len, dilation, h, e).transpose(0, 2, 1, 3, 4)
        return jnp.pad(t, ((0, 0), (0, 0), (0, m_pad - m_len), (0, 0), (0, 0)))

    def band(t):
        tp = jnp.pad(t, ((0, 0), (0, 0), (C_BLOCK, 0), (0, 0), (0, 0)))
        prev = tp[:, :, :m_pad].reshape(n, dilation, nblk, C_BLOCK, h, e)
        cur = tp[:, :, C_BLOCK:].reshape(n, dilation, nblk, C_BLOCK, h, e)
        return jnp.concatenate([prev, cur], axis=3)

    qb = to_sub(q).reshape(n, dilation, nblk, C_BLOCK, h, e).astype(jnp.float32)
    kb = band(to_sub(k)).astype(jnp.float32)
    vb = band(to_sub(v)).astype(jnp.float32)
    scores = jnp.einsum('nrbqhe,nrbkhe->nrbhqk', qb, kb) * (C_HEAD_DIM ** -0.5)
    qi = jnp.arange(C_BLOCK)[:, None]
    kj = jnp.arange(2 * C_BLOCK)[None, :]
    dist = C_BLOCK + qi - kj
    key_idx = jnp.arange(nblk)[:, None, None] * C_BLOCK - C_BLOCK + kj
    mask = (dist >= 0) & (dist <= reach) & (key_idx >= 0)
    scores = jnp.where(mask[None, None, :, None], scores, -jnp.inf)
    mx = jnp.max(scores, axis=-1, keepdims=True)
    p = jnp.exp(scores - mx)
    l = jnp.sum(p, axis=-1, keepdims=True)
    o = jnp.einsum('nrbhqk,nrbkhe->nrbqhe', p, vb) / jnp.swapaxes(l, 3, 4)
    lse = jnp.swapaxes((mx + jnp.log(l))[..., 0], 3, 4)
    o = o.reshape(n, dilation, m_pad, h, e)[:, :, :m_len].transpose(0, 2, 1, 3, 4).reshape(n, s_len, h, e)
    lse = lse.reshape(n, dilation, m_pad, h)[:, :, :m_len].transpose(0, 2, 1, 3).reshape(n, s_len, h)
    return o, lse


def dilated_attention_sample(q, k_all, v_all, window, dilation, hist_len):
    t = q.shape[1]
    reach = window // dilation
    idx = hist_len + jnp.arange(t)[:, None] - dilation * jnp.arange(reach + 1)[None, :]
    valid = idx >= 0
    idx_c = jnp.maximum(idx, 0)
    kg = k_all[:, idx_c].astype(jnp.float32)
    vg = v_all[:, idx_c].astype(jnp.float32)
    scores = jnp.einsum('nthe,ntjhe->nthj', q.astype(jnp.float32), kg) * (C_HEAD_DIM ** -0.5)
    scores = jnp.where(valid[None, :, None, :], scores, -jnp.inf)
    mx = jnp.max(scores, axis=-1, keepdims=True)
    p = jnp.exp(scores - mx)
    l = jnp.sum(p, axis=-1, keepdims=True)
    o = jnp.einsum('nthj,ntjhe->nthe', p, vg) / l
    return o, (mx + jnp.log(l))[..., 0]


def dilated_attention_mix(h, hist_k, hist_v, w_qkv, w_o):
    n, t, _ = h.shape
    q, k, v = [a.reshape(n, t, C_HEADS, C_HEAD_DIM) for a in jnp.split(h @ w_qkv, 3, axis=-1)]
    if hist_k is None:
        res = [dilated_attention_prompt(q, k, v, w, d) for (w, d) in C_PATTERNS]
        keep = min(C_MAX_WINDOW, t)
        new_k, new_v = k[:, t - keep:], v[:, t - keep:]
    else:
        hist_len = hist_k.shape[1]
        k_all = jnp.concatenate([hist_k, k], axis=1)
        v_all = jnp.concatenate([hist_v, v], axis=1)
        res = [dilated_attention_sample(q, k_all, v_all, w, d, hist_len) for (w, d) in C_PATTERNS]
        new_k, new_v = k_all[:, t:], v_all[:, t:]
    outs = jnp.stack([r[0] for r in res])
    wts = jax.nn.softmax(jnp.stack([r[1] for r in res]), axis=0)
    o = jnp.einsum('pnth,pnthe->nthe', wts, outs).astype(h.dtype)
    return o.reshape(n, t, C_WIDTH) @ w_o, new_k, new_v


def swiglu(h, w_gate_up, w_down):
    g, u = jnp.split(h @ w_gate_up, 2, axis=-1)
    return (jax.nn.silu(g) * u) @ w_down


def setup_inputs(seed: int = 0) -> dict:
    key = jax.random.key(seed)
    ks = jax.random.split(key, 24)

    def nrm(k, shape, scale):
        return jax.random.normal(k, shape, jnp.float32) * scale

    c_len = min(C_MAX_WINDOW, PAST_LEN)
    return {
        'x_prompt': nrm(ks[0], (BATCH, SEQ, D_MODEL), 1.0),
        'x_sample': nrm(ks[1], (DEC_BATCH, DEC_SEQ, D_MODEL), 1.0),
        'state_a_conv': nrm(ks[2], (N_EVEN, DEC_BATCH, A_CONV_WIDTH - 1, A_WIDTH), 0.5),
        'cache_c_k': nrm(ks[3], (N_ODD, DEC_BATCH, c_len, C_HEADS, C_HEAD_DIM), 1.0),
        'cache_c_v': nrm(ks[4], (N_ODD, DEC_BATCH, c_len, C_HEADS, C_HEAD_DIM), 1.0),
        'ln_mix_even': 1.0 + nrm(ks[5], (N_EVEN, D_MODEL), 0.02),
        'w_in_even': nrm(ks[6], (N_EVEN, D_MODEL, 2 * A_WIDTH + 2 * B_WIDTH), D_MODEL ** -0.5),
        'a_conv_w': nrm(ks[7], (N_EVEN, A_CONV_WIDTH, A_WIDTH), A_CONV_WIDTH ** -0.5),
        'a_conv_b': nrm(ks[8], (N_EVEN, A_WIDTH), 0.02),
        'a_norm_g': 1.0 + nrm(ks[9], (N_EVEN, A_WIDTH), 0.02),
        'a_norm_b': nrm(ks[10], (N_EVEN, A_WIDTH), 0.02),
        'b_norm_g': 1.0 + nrm(ks[11], (N_EVEN, B_WIDTH), 0.02),
        'b_norm_b': nrm(ks[12], (N_EVEN, B_WIDTH), 0.02),
        'b_spatial_w': nrm(ks[13], (N_EVEN, B_GROUPS, B_CHUNK, B_CHUNK), B_CHUNK ** -0.5),
        'b_spatial_b': 1.0 + nrm(ks[14], (N_EVEN, B_GROUPS, B_CHUNK), 0.02),
        'w_out_even': nrm(ks[15], (N_EVEN, A_WIDTH + B_WIDTH, D_MODEL), (A_WIDTH + B_WIDTH) ** -0.5),
        'ln_mix_odd': 1.0 + nrm(ks[16], (N_ODD, D_MODEL), 0.02),
        'w_qkv_odd': nrm(ks[17], (N_ODD, D_MODEL, 3 * C_WIDTH), D_MODEL ** -0.5),
        'w_o_odd': nrm(ks[18], (N_ODD, C_WIDTH, D_MODEL), C_WIDTH ** -0.5),
        'ln_ffn': 1.0 + nrm(ks[19], (DEPTH, D_MODEL), 0.02),
        'w_gate_up': nrm(ks[20], (DEPTH, D_MODEL, 2 * FFN_HIDDEN), D_MODEL ** -0.5),
        'w_down': nrm(ks[21], (DEPTH, FFN_HIDDEN, D_MODEL), FFN_HIDDEN ** -0.5),
        'ln_final': 1.0 + nrm(ks[22], (D_MODEL,), 0.02),
    }


def reference(x_prompt, x_sample, state_a_conv, cache_c_k, cache_c_v, ln_mix_even, w_in_even,
              a_conv_w, a_conv_b, a_norm_g, a_norm_b, b_norm_g, b_norm_b, b_spatial_w,
              b_spatial_b, w_out_even, ln_mix_odd, w_qkv_odd, w_o_odd, ln_ffn, w_gate_up,
              w_down, ln_final):
    xp, xs = x_prompt, x_sample
    a_hist_p, a_hist_s, b_v_s = [], [], []
    c_k_p, c_v_p, c_k_s, c_v_s = [], [], [], []
    for i in range(DEPTH):
        j = i // 2
        if i % 2 == 0:
            ew = (w_in_even[j], a_conv_w[j], a_conv_b[j], a_norm_g[j], a_norm_b[j],
                  b_norm_g[j], b_norm_b[j], b_spatial_w[j], b_spatial_b[j], w_out_even[j])
            zero_hist = jnp.zeros((xp.shape[0], A_CONV_WIDTH - 1, A_WIDTH), xp.dtype)
            yp, hp_new, _ = conv_gating_mix(rms_norm(xp, ln_mix_even[j]), zero_hist, *ew)
            ys, hs_new, vs_new = conv_gating_mix(rms_norm(xs, ln_mix_even[j]), state_a_conv[j], *ew)
            a_hist_p.append(hp_new)
            a_hist_s.append(hs_new)
            b_v_s.append(vs_new)
        else:
            yp, kp, vp = dilated_attention_mix(rms_norm(xp, ln_mix_odd[j]), None, None,
                                               w_qkv_odd[j], w_o_odd[j])
            ys, ks_new, vs_new = dilated_attention_mix(rms_norm(xs, ln_mix_odd[j]), cache_c_k[j],
                                                       cache_c_v[j], w_qkv_odd[j], w_o_odd[j])
            c_k_p.append(kp)
            c_v_p.append(vp)
            c_k_s.append(ks_new)
            c_v_s.append(vs_new)
        xp = xp + yp
        xs = xs + ys
        xp = xp + swiglu(rms_norm(xp, ln_ffn[i]), w_gate_up[i], w_down[i])
        xs = xs + swiglu(rms_norm(xs, ln_ffn[i]), w_gate_up[i], w_down[i])
    y_prompt = rms_norm(xp, ln_final)
    y_sample = rms_norm(xs, ln_final)
    return (y_prompt, y_sample, jnp.stack(a_hist_p), jnp.stack(a_hist_s), jnp.stack(b_v_s),
            jnp.stack(c_k_p), jnp.stack(c_v_p), jnp.stack(c_k_s), jnp.stack(c_v_s))
```

```python
import functools

import jax
import jax.numpy as jnp
from jax import lax
from jax.experimental import pallas as pl
from jax.experimental.pallas import tpu as pltpu

F32 = jnp.float32
BF16 = jnp.bfloat16

LANES = 128
VMEM_BYTES = 64 * 1024 * 1024
VMEM_LIMIT = VMEM_BYTES - 8 * 1024 * 1024

RMS_EPS = 1e-6
LN_EPS = 1e-5

A_GROUPS = 16
B_GROUPS = 16
B_CHUNK = 128
C_HEADS = 32
C_PATTERNS = ((128, 1), (512, 4), (2048, 16))
C_BLOCK = 128

ROW_TILE = 1040
N_TILE = 512
GATE_TILE = 256


def _params(semantics):
    return pltpu.CompilerParams(dimension_semantics=semantics, vmem_limit_bytes=VMEM_LIMIT)


def _rms_kernel(x_ref, g_ref, o_ref):
    x = x_ref[...]
    y = x * lax.rsqrt(jnp.mean(x * x, axis=-1, keepdims=True) + RMS_EPS)
    o_ref[...] = (y * g_ref[...]).astype(o_ref.dtype)


def _rmsnorm(x, gains, layer, out_dtype, *, rows, row_tile, first_block=0):
    d = x.shape[1]
    return pl.pallas_call(
        _rms_kernel,
        grid=(rows // row_tile,),
        in_specs=[
            pl.BlockSpec((row_tile, d), lambda i: (i + first_block, 0)),
            pl.BlockSpec((None, 1, d), lambda i: (layer, 0, 0)),
        ],
        out_specs=pl.BlockSpec((row_tile, d), lambda i: (i, 0)),
        out_shape=jax.ShapeDtypeStruct((rows, d), out_dtype),
        compiler_params=_params(("parallel",)),
    )(x, gains)


def _mm_kernel(*refs, nk, tk, has_res):
    a_ref, b_ref = refs[0], refs[1]
    res_ref = refs[2] if has_res else None
    o_ref = refs[3] if has_res else refs[2]

    def finish(acc):
        if has_res:
            acc = acc + res_ref[...]
        o_ref[...] = acc.astype(o_ref.dtype)

    if nk == 1:
        finish(jnp.dot(a_ref[...], b_ref[...].astype(BF16), preferred_element_type=F32))
        return

    acc_ref = refs[-1]
    k = pl.program_id(2)
    for kk in range(nk):
        @pl.when(k == kk)
        def _(kk=kk):
            part = jnp.dot(a_ref[:, kk * tk:(kk + 1) * tk], b_ref[...].astype(BF16),
                           preferred_element_type=F32)
            if kk == 0:
                acc_ref[...] = part
            elif kk < nk - 1:
                acc_ref[...] += part
            else:
                finish(acc_ref[...] + part)


def _matmul(a, w, layer, *, n, n_off=0, tm=ROW_TILE, tn=N_TILE, nk=1, res=None, out_dtype=F32):
    m, kdim = a.shape
    tk = kdim // nk
    joff = n_off // tn
    has_res = res is not None
    in_specs = [
        pl.BlockSpec((tm, kdim), lambda i, j, k: (i, 0), pipeline_mode=pl.Buffered(1)),
        pl.BlockSpec((None, tk, tn), lambda i, j, k: (layer, k, j + joff)),
    ]
    args = [a, w]
    if has_res:
        in_specs.append(pl.BlockSpec((tm, tn), lambda i, j, k: (i, j)))
        args.append(res)
    scratch = [pltpu.VMEM((tm, tn), F32)] if nk > 1 else []
    return pl.pallas_call(
        functools.partial(_mm_kernel, nk=nk, tk=tk, has_res=has_res),
        grid=(m // tm, n // tn, nk),
        in_specs=in_specs,
        out_specs=pl.BlockSpec((tm, tn), lambda i, j, k: (i, j)),
        out_shape=jax.ShapeDtypeStruct((m, n), out_dtype),
        scratch_shapes=scratch,
        compiler_params=_params(("parallel", "parallel", "arbitrary")),
    )(*args)


def _mm_gated_kernel(a_ref, bg_ref, bu_ref, o_ref):
    a = a_ref[...]
    g = jnp.dot(a, bg_ref[...].astype(BF16), preferred_element_type=F32)
    u = jnp.dot(a, bu_ref[...].astype(BF16), preferred_element_type=F32)
    o_ref[...] = (g * jax.nn.sigmoid(g) * u).astype(o_ref.dtype)


def _matmul_gated(a, w, layer, *, half, tm=ROW_TILE, tn=GATE_TILE):
    m, kdim = a.shape
    nj = half // tn
    return pl.pallas_call(
        _mm_gated_kernel,
        grid=(m // tm, nj),
        in_specs=[
            pl.BlockSpec((tm, kdim), lambda i, j: (i, 0), pipeline_mode=pl.Buffered(1)),
            pl.BlockSpec((None, kdim, tn), lambda i, j: (layer, 0, j)),
            pl.BlockSpec((None, kdim, tn), lambda i, j: (layer, 0, j + nj)),
        ],
        out_specs=pl.BlockSpec((tm, tn), lambda i, j: (i, j)),
        out_shape=jax.ShapeDtypeStruct((m, half), BF16),
        compiler_params=_params(("parallel", "parallel")),
    )(a, w, w)


def _gelu(x):
    return 0.5 * x * (1.0 + lax.erf(x * (2.0 ** -0.5)))


def _layer_norm(x, g, b):
    xc = x - jnp.mean(x, axis=-1, keepdims=True)
    y = xc * lax.rsqrt(jnp.mean(xc * xc, axis=-1, keepdims=True) + LN_EPS)
    return y * g + b


def _silu(x):
    return x * jax.nn.sigmoid(x)


def _mixa_prompt_kernel(val_ref, gate_ref, w_ref, cb_ref, g_ref, b_ref, a_ref, st_ref, ext_ref,
                        *, seq, kw, hist_pad, chunk):
    ext_ref[0:hist_pad, :] = jnp.zeros((hist_pad, LANES), F32)
    ext_ref[hist_pad:hist_pad + seq, :] = val_ref[...] * jax.nn.sigmoid(gate_ref[...])
    st_ref[...] = ext_ref[hist_pad + seq - (kw - 1):hist_pad + seq, :]
    off = hist_pad - (kw - 1)
    for c in range(seq // chunk):
        t0 = c * chunk
        acc = w_ref[0:1, :] * ext_ref[t0 + off:t0 + off + chunk, :]
        for k in range(1, kw):
            acc = acc + w_ref[k:k + 1, :] * ext_ref[t0 + off + k:t0 + off + k + chunk, :]
        y = _layer_norm(acc + cb_ref[...], g_ref[...], b_ref[...])
        a_ref[t0:t0 + chunk, :] = _silu(y).astype(a_ref.dtype)


def _mixa_prompt(z, conv_w, conv_b, norm_g, norm_b, layer, *, batch, seq, width):
    kw = conv_w.shape[1]
    nc = width // LANES
    hist_pad = 32
    vec = lambda: pl.BlockSpec((None, 1, LANES), lambda b, c: (layer, 0, c))
    return pl.pallas_call(
        functools.partial(_mixa_prompt_kernel, seq=seq, kw=kw, hist_pad=hist_pad, chunk=128),
        grid=(batch, nc),
        in_specs=[
            pl.BlockSpec((seq, LANES), lambda b, c: (b, c)),
            pl.BlockSpec((seq, LANES), lambda b, c: (b, c + nc)),
            pl.BlockSpec((None, kw, LANES), lambda b, c: (layer, 0, c)),
            vec(), vec(), vec(),
        ],
        out_specs=[
            pl.BlockSpec((seq, LANES), lambda b, c: (b, c)),
            pl.BlockSpec((None, kw - 1, LANES), lambda b, c: (b, 0, c)),
        ],
        out_shape=[
            jax.ShapeDtypeStruct((batch * seq, width), BF16),
            jax.ShapeDtypeStruct((batch, kw - 1, width), F32),
        ],
        scratch_shapes=[pltpu.VMEM((hist_pad + seq, LANES), F32)],
        compiler_params=_params(("parallel", "parallel")),
    )(z, z, conv_w, conv_b, norm_g, norm_b)


def _mixb_prompt_kernel(u_ref, v_ref, ng_ref, nb_ref, ws_ref, bst_ref, o_ref, *, groups):
    v = _layer_norm(_gelu(v_ref[...]), ng_ref[...], nb_ref[...]).astype(BF16)
    n = ws_ref.shape[-1]
    causal = lax.broadcasted_iota(jnp.int32, (n, n), 1) <= lax.broadcasted_iota(jnp.int32, (n, n), 0)
    for g in range(groups):
        cols = slice(g * LANES, (g + 1) * LANES)
        w_m = jnp.where(causal, ws_ref[g], 0.0).astype(BF16)
        s = jnp.dot(w_m, v[:, cols], preferred_element_type=F32) + bst_ref[:, g:g + 1]
        o_ref[:, cols] = (_gelu(u_ref[:, cols]) * s).astype(o_ref.dtype)


def _mixb_prompt(z, norm_g, norm_b, spatial_w, spatial_bt, layer, *, rows, width):
    groups, chunk = spatial_w.shape[1], spatial_w.shape[2]
    vec = lambda: pl.BlockSpec((None, 1, width), lambda s: (layer, 0, 0))
    return pl.pallas_call(
        functools.partial(_mixb_prompt_kernel, groups=groups),
        grid=(rows // chunk,),
        in_specs=[
            pl.BlockSpec((chunk, width), lambda s: (s, 2)),
            pl.BlockSpec((chunk, width), lambda s: (s, 3)),
            vec(), vec(),
            pl.BlockSpec((None, groups, chunk, chunk), lambda s: (layer, 0, 0, 0)),
            pl.BlockSpec((None, chunk, groups), lambda s: (layer, 0, 0)),
        ],
        out_specs=pl.BlockSpec((chunk, width), lambda s: (s, 0)),
        out_shape=jax.ShapeDtypeStruct((rows, width), BF16),
        compiler_params=_params(("parallel",)),
    )(z, z, norm_g, norm_b, spatial_w, spatial_bt)


def _mix_sample_kernel(val_ref, gate_ref, u_ref, v_ref, st_ref, cw_ref, cb_ref, ag_ref, ab_ref,
                       bg_ref, bb_ref, sw_ref, sb_ref, y_ref, ain_ref, vout_ref, *, kw, groups, width):
    a_in = val_ref[...] * jax.nn.sigmoid(gate_ref[...])
    ain_ref[...] = a_in
    acc = cw_ref[0:1, :] * st_ref[0]
    for k in range(1, kw - 1):
        acc = acc + cw_ref[k:k + 1, :] * st_ref[k]
    acc = acc + cw_ref[kw - 1:kw, :] * a_in + cb_ref[...]
    for g in range(groups):
        cols = slice(g * LANES, (g + 1) * LANES)
        y = _layer_norm(acc[:, cols], ag_ref[:, cols], ab_ref[:, cols])
        y_ref[:, cols] = _silu(y)
    v = _layer_norm(_gelu(v_ref[...]), bg_ref[...], bb_ref[...])
    vout_ref[...] = v
    y_ref[:, width:2 * width] = _gelu(u_ref[...]) * (sw_ref[...] * v + sb_ref[...])


def _mix_sample(z, state_t, conv_w, conv_b, a_g, a_b, b_g, b_b, s_w0, s_b0, layer, *, row0, n, width):
    kw = conv_w.shape[1]
    rb = row0 // n
    zspec = lambda c: pl.BlockSpec((n, width), lambda i: (rb, c))
    vec = lambda: pl.BlockSpec((None, 1, width), lambda i: (layer, 0, 0))
    return pl.pallas_call(
        functools.partial(_mix_sample_kernel, kw=kw, groups=A_GROUPS, width=width),
        grid=(1,),
        in_specs=[
            zspec(0), zspec(1), zspec(2), zspec(3),
            pl.BlockSpec((kw - 1, n, width), lambda i: (0, 0, 0)),
            pl.BlockSpec((None, kw, width), lambda i: (layer, 0, 0)),
            vec(), vec(), vec(), vec(), vec(), vec(), vec(),
        ],
        out_specs=[
            pl.BlockSpec((n, 2 * width), lambda i: (0, 0)),
            pl.BlockSpec((n, width), lambda i: (0, 0)),
            pl.BlockSpec((n, width), lambda i: (0, 0)),
        ],
        out_shape=[
            jax.ShapeDtypeStruct((n, 2 * width), F32),
            jax.ShapeDtypeStruct((n, width), F32),
            jax.ShapeDtypeStruct((n, width), F32),
        ],
        compiler_params=_params(("arbitrary",)),
    )(z, z, z, z, state_t, conv_w, conv_b, a_g, a_b, b_g, b_b, s_w0, s_b0)


def _rows(start, size, stride):
    return pl.ds(start, size) if stride == 1 else pl.ds(start, size, stride=stride)


def _attn_prompt_kernel(q_ref, k_ref, v_ref, o_ref, *scratch, patterns, blk, scale):
    seq = q_ref.shape[0]
    npat = len(patterns)
    acc_refs, m_refs, l_refs = scratch[:npat], scratch[npat:2 * npat], scratch[2 * npat:]

    for p, (window, dil) in enumerate(patterns):
        reach = window // dil
        sub_len = seq // dil
        nblk = sub_len // blk
        band = 2 * blk if nblk > 1 else blk
        dist0 = (lax.broadcasted_iota(jnp.int32, (blk, band), 0)
                 - lax.broadcasted_iota(jnp.int32, (blk, band), 1))

        def block(it, carry, p=p, dil=dil, reach=reach, nblk=nblk, band=band, dist0=dist0):
            r = it // nblk
            b = it % nblk
            kb = jnp.maximum(b - 1, 0) if band > blk else 0
            q_rows = _rows(r + dil * blk * b, blk, dil)
            k_rows = _rows(r + dil * blk * kb, band, dil)
            qv = q_ref[q_rows, :].astype(BF16)
            kv = k_ref[k_rows, :].astype(BF16)
            vv = v_ref[k_rows, :].astype(BF16)
            s = lax.dot_general(qv, kv, (((1,), (1,)), ((), ())), preferred_element_type=F32) * scale
            dist = dist0 + blk * (b - kb)
            s = jnp.where((dist >= 0) & (dist <= reach), s, -jnp.inf)
            m = jnp.max(s, axis=-1, keepdims=True)
            e = jnp.exp(s - m)
            l = jnp.sum(e, axis=-1, keepdims=True)
            acc_refs[p][q_rows, :] = jnp.dot(e.astype(BF16), vv, preferred_element_type=F32)
            m_refs[p][q_rows, :] = jnp.broadcast_to(m, (blk, LANES))
            l_refs[p][q_rows, :] = jnp.broadcast_to(l, (blk, LANES))
            return carry

        lax.fori_loop(0, dil * nblk, block, 0)

    def merge(c, carry):
        rows = pl.ds(pl.multiple_of(c * blk, blk), blk)
        ms = [m_ref[rows, :] for m_ref in m_refs]
        top = functools.reduce(jnp.maximum, ms)
        ws = [jnp.exp(m - top) for m in ms]
        den = functools.reduce(lambda x, y: x + y, [w * l_ref[rows, :] for w, l_ref in zip(ws, l_refs)])
        num = functools.reduce(lambda x, y: x + y, [w * a_ref[rows, :] for w, a_ref in zip(ws, acc_refs)])
        o_ref[rows, :] = (num / den).astype(o_ref.dtype)
        return carry

    lax.fori_loop(0, seq // blk, merge, 0)


def _attn_prompt(q, k, v, *, batch, seq, heads):
    e = q.shape[1] // heads
    assert e == LANES
    for window, dil in C_PATTERNS:
        assert window // dil <= C_BLOCK and (seq // dil) % C_BLOCK == 0
    spec = pl.BlockSpec((seq, e), lambda b, h: (b, h))
    nscratch = 3 * len(C_PATTERNS)
    return pl.pallas_call(
        functools.partial(_attn_prompt_kernel, patterns=C_PATTERNS, blk=C_BLOCK, scale=e ** -0.5),
        grid=(batch, heads),
        in_specs=[spec, spec, spec],
        out_specs=spec,
        out_shape=jax.ShapeDtypeStruct((batch * seq, heads * e), BF16),
        scratch_shapes=[pltpu.VMEM((seq, e), F32) for _ in range(nscratch)],
        compiler_params=_params(("parallel", "parallel")),
    )(q, k, v)


def _attn_sample_kernel(q_ref, kn_ref, vn_ref, kc_ref, vc_ref, o_ref, *, patterns, scale):
    n = pl.program_id(1)
    hist = kc_ref.shape[0]
    row = pl.ds(n, 1)
    q = q_ref[row, :]
    s_new = jnp.sum(q * kn_ref[row, :], axis=-1, keepdims=True) * scale
    scores = []
    for window, dil in patterns:
        reach = window // dil
        kp = kc_ref[_rows(hist - reach * dil, reach, dil), :]
        scores.append(jnp.sum(kp * q, axis=-1, keepdims=True) * scale)
    top = s_new
    for s in scores:
        top = jnp.maximum(top, jnp.max(s, axis=0, keepdims=True))
    w_new = len(patterns) * jnp.exp(s_new - top)
    den = w_new
    num = w_new * vn_ref[row, :]
    for (window, dil), s in zip(patterns, scores):
        reach = window // dil
        e = jnp.exp(s - top)
        den = den + jnp.sum(e, axis=0, keepdims=True)
        vp = vc_ref[_rows(hist - reach * dil, reach, dil), :]
        num = num + jnp.sum(e * vp, axis=0, keepdims=True)
    o_ref[row, :] = num / den


def _attn_sample(q, k, v, cache_k, cache_v, *, row0, n, heads):
    e = q.shape[1] // heads
    hist = cache_k.shape[1]
    for window, dil in C_PATTERNS:
        assert window <= hist
    rb = row0 // n
    new = pl.BlockSpec((n, e), lambda h, i: (rb, h))
    cache = pl.BlockSpec((None, hist, e), lambda h, i: (i, 0, h))
    return pl.pallas_call(
        functools.partial(_attn_sample_kernel, patterns=C_PATTERNS, scale=e ** -0.5),
        grid=(heads, n),
        in_specs=[new, new, new, cache, cache],
        out_specs=pl.BlockSpec((n, e), lambda h, i: (0, h)),
        out_shape=jax.ShapeDtypeStruct((n, heads * e), F32),
        compiler_params=_params(("parallel", "arbitrary")),
    )(q, k, v, cache_k, cache_v)


def kernel(x_prompt, x_sample, state_a_conv, cache_c_k, cache_c_v, ln_mix_even, w_in_even, a_conv_w,
           a_conv_b, a_norm_g, a_norm_b, b_norm_g, b_norm_b, b_spatial_w, b_spatial_b, w_out_even,
           ln_mix_odd, w_qkv_odd, w_o_odd, ln_ffn, w_gate_up, w_down, ln_final):
    batch, seq, d = x_prompt.shape
    nsamp = x_sample.shape[0] * x_sample.shape[1]
    depth = ln_ffn.shape[0]
    np_rows = batch * seq
    rows_all = pl.cdiv(np_rows + nsamp, ROW_TILE) * ROW_TILE
    pad = rows_all - np_rows - nsamp
    width = a_conv_w.shape[-1]
    ffn = w_down.shape[1]

    def stack_rows(p, s):
        return jnp.concatenate([p, s.astype(p.dtype), jnp.zeros((pad, p.shape[1]), p.dtype)], axis=0)

    vec3 = lambda t: t.reshape(t.shape[0], 1, t.shape[-1])
    x = stack_rows(x_prompt.reshape(np_rows, d), x_sample.reshape(nsamp, d))
    rms_tile = rows_all // 26

    a_hist_p, a_hist_s, b_v_s = [], [], []
    c_k_p, c_v_p, c_k_s, c_v_s = [], [], [], []
    for i in range(depth):
        j = i // 2
        if i % 2 == 0:
            h = _rmsnorm(x, vec3(ln_mix_even), j, BF16, rows=rows_all, row_tile=rms_tile)
            z = _matmul(h, w_in_even, j, n=w_in_even.shape[-1])
            a_p, hist_p = _mixa_prompt(z, a_conv_w, vec3(a_conv_b), vec3(a_norm_g), vec3(a_norm_b), j,
                                       batch=batch, seq=seq, width=width)
            b_p = _mixb_prompt(z, vec3(b_norm_g), vec3(b_norm_b), b_spatial_w,
                               jnp.swapaxes(b_spatial_b, 1, 2), j, rows=np_rows, width=width)
            s_w0 = jnp.repeat(b_spatial_w[:, :, 0, 0], width // B_GROUPS, axis=-1)
            s_b0 = jnp.repeat(b_spatial_b[:, :, 0], width // B_GROUPS, axis=-1)
            y_s, a_in_s, v_s = _mix_sample(
                z, jnp.swapaxes(state_a_conv[j], 0, 1), a_conv_w, vec3(a_conv_b), vec3(a_norm_g),
                vec3(a_norm_b), vec3(b_norm_g), vec3(b_norm_b), vec3(s_w0), vec3(s_b0), j,
                row0=np_rows, n=nsamp, width=width)
            y = stack_rows(jnp.concatenate([a_p, b_p], axis=1), y_s)
            x = _matmul(y, w_out_even, j, n=d, res=x)
            a_hist_p.append(hist_p)
            a_hist_s.append(jnp.concatenate([state_a_conv[j][:, 1:], a_in_s[:, None, :]], axis=1))
            b_v_s.append(v_s.reshape(x_sample.shape[0], x_sample.shape[1], width))
        else:
            h = _rmsnorm(x, vec3(ln_mix_odd), j, BF16, rows=rows_all, row_tile=rms_tile)
            q, k, v = [_matmul(h, w_qkv_odd, j, n=d, n_off=c * d) for c in range(3)]
            o_p = _attn_prompt(q, k, v, batch=batch, seq=seq, heads=C_HEADS)
            hist = cache_c_k.shape[2]
            ck = cache_c_k[j].reshape(nsamp, hist, d)
            cv = cache_c_v[j].reshape(nsamp, hist, d)
            o_s = _attn_sample(q, k, v, ck, cv, row0=np_rows, n=nsamp, heads=C_HEADS)
            x = _matmul(stack_rows(o_p, o_s), w_o_odd, j, n=d, res=x)
            e = d // C_HEADS
            keep = min(C_PATTERNS[-1][0], seq)
            c_k_p.append(k[:np_rows].reshape(batch, seq, C_HEADS, e)[:, seq - keep:])
            c_v_p.append(v[:np_rows].reshape(batch, seq, C_HEADS, e)[:, seq - keep:])
            k_s = k[np_rows:np_rows + nsamp].reshape(nsamp, 1, C_HEADS, e)
            v_s = v[np_rows:np_rows + nsamp].reshape(nsamp, 1, C_HEADS, e)
            c_k_s.append(jnp.concatenate([cache_c_k[j][:, 1:], k_s], axis=1))
            c_v_s.append(jnp.concatenate([cache_c_v[j][:, 1:], v_s], axis=1))
        h = _rmsnorm(x, vec3(ln_ffn), i, BF16, rows=rows_all, row_tile=rms_tile)
        hid = _matmul_gated(h, w_gate_up, i, half=ffn)
        x = _matmul(hid, w_down, i, n=d, tn=GATE_TILE, nk=2, res=x)

    g_final = ln_final.reshape(1, 1, d)
    y_prompt = _rmsnorm(x, g_final, 0, F32, rows=np_rows, row_tile=256)
    y_sample = _rmsnorm(x, g_final, 0, F32, rows=nsamp, row_tile=nsamp, first_block=np_rows // nsamp)
    return (y_prompt.reshape(batch, seq, d), y_sample.reshape(x_sample.shape),
            jnp.stack(a_hist_p), jnp.stack(a_hist_s), jnp.stack(b_v_s),
            jnp.stack(c_k_p), jnp.stack(c_v_p), jnp.stack(c_k_s), jnp.stack(c_v_s))
```

```python
import functools

import jax
import jax.numpy as jnp
from jax import lax
from jax.experimental import pallas as pl
from jax.experimental.pallas import tpu as pltpu

F32 = jnp.float32
BF16 = jnp.bfloat16

LANES = 128
VMEM_BYTES = 64 * 1024 * 1024
VMEM_LIMIT = VMEM_BYTES - 8 * 1024 * 1024

RMS_EPS = 1e-6
LN_EPS = 1e-5

A_GROUPS = 16
B_GROUPS = 16
B_CHUNK = 128
C_HEADS = 32
C_PATTERNS = ((128, 1), (512, 4), (2048, 16))
C_BLOCK = 128

ROW_TILE = 1040
N_TILE = 512
GATE_TILE = 256
RMS_TILE = 320
RMS_TILE_OUT = 256


def _params(semantics):
    return pltpu.CompilerParams(dimension_semantics=semantics, vmem_limit_bytes=VMEM_LIMIT)


def _rms_kernel(x_ref, g_ref, o_ref):
    x = x_ref[...]
    y = x * lax.rsqrt(jnp.mean(x * x, axis=-1, keepdims=True) + RMS_EPS)
    o_ref[...] = (y * g_ref[...]).astype(o_ref.dtype)


def _rmsnorm(x, gains, layer, out_dtype, *, rows, row_tile, first_block=0):
    d = x.shape[1]
    return pl.pallas_call(
        _rms_kernel,
        grid=(rows // row_tile,),
        in_specs=[
            pl.BlockSpec((row_tile, d), lambda i: (i + first_block, 0)),
            pl.BlockSpec((None, 1, d), lambda i: (layer, 0, 0)),
        ],
        out_specs=pl.BlockSpec((row_tile, d), lambda i: (i, 0)),
        out_shape=jax.ShapeDtypeStruct((rows, d), out_dtype),
        compiler_params=_params(("parallel",)),
    )(x, gains)


def _mm_kernel(*refs, a_widths, nk, tk, has_res):
    na = len(a_widths)
    a_refs, b_ref = refs[:na], refs[na]
    res_ref = refs[na + 1] if has_res else None
    o_ref = refs[na + 2] if has_res else refs[na + 1]

    def finish(acc):
        if has_res:
            acc = acc + res_ref[...]
        o_ref[...] = acc.astype(o_ref.dtype)

    if nk == 1:
        acc, k0 = None, 0
        for a_ref, kw in zip(a_refs, a_widths):
            part = jnp.dot(a_ref[...], b_ref[k0:k0 + kw, :].astype(BF16), preferred_element_type=F32)
            acc = part if acc is None else acc + part
            k0 += kw
        finish(acc)
        return

    (a_ref,), acc_ref = a_refs, refs[-1]
    k = pl.program_id(2)
    for kk in range(nk):
        @pl.when(k == kk)
        def _(kk=kk):
            part = jnp.dot(a_ref[:, kk * tk:(kk + 1) * tk], b_ref[...].astype(BF16),
                           preferred_element_type=F32)
            if kk == 0:
                acc_ref[...] = part
            elif kk < nk - 1:
                acc_ref[...] += part
            else:
                finish(acc_ref[...] + part)


def _matmul(a_parts, w, layer, *, n, n_off=0, tm=ROW_TILE, tn=N_TILE, nk=1, res=None, out_dtype=F32):
    m = a_parts[0].shape[0]
    a_widths = tuple(a.shape[1] for a in a_parts)
    kdim = sum(a_widths)
    assert nk == 1 or len(a_parts) == 1
    tk = kdim // nk
    joff = n_off // tn
    has_res = res is not None
    in_specs = [pl.BlockSpec((tm, kw), lambda i, j, k: (i, 0), pipeline_mode=pl.Buffered(1)) for kw in a_widths]
    in_specs.append(pl.BlockSpec((None, tk, tn), lambda i, j, k: (layer, k, j + joff)))
    args = [*a_parts, w]
    if has_res:
        in_specs.append(pl.BlockSpec((tm, tn), lambda i, j, k: (i, j)))
        args.append(res)
    scratch = [pltpu.VMEM((tm, tn), F32)] if nk > 1 else []
    return pl.pallas_call(
        functools.partial(_mm_kernel, a_widths=a_widths, nk=nk, tk=tk, has_res=has_res),
        grid=(m // tm, n // tn, nk),
        in_specs=in_specs,
        out_specs=pl.BlockSpec((tm, tn), lambda i, j, k: (i, j)),
        out_shape=jax.ShapeDtypeStruct((m, n), out_dtype),
        scratch_shapes=scratch,
        compiler_params=_params(("parallel", "parallel", "arbitrary")),
    )(*args)


def _mm_gated_kernel(a_ref, bg_ref, bu_ref, o_ref):
    a = a_ref[...]
    g = jnp.dot(a, bg_ref[...].astype(BF16), preferred_element_type=F32)
    u = jnp.dot(a, bu_ref[...].astype(BF16), preferred_element_type=F32)
    o_ref[...] = (g * jax.nn.sigmoid(g) * u).astype(o_ref.dtype)


def _matmul_gated(a, w, layer, *, half, tm=ROW_TILE, tn=GATE_TILE):
    m, kdim = a.shape
    nj = half // tn
    return pl.pallas_call(
        _mm_gated_kernel,
        grid=(m // tm, nj),
        in_specs=[
            pl.BlockSpec((tm, kdim), lambda i, j: (i, 0), pipeline_mode=pl.Buffered(1)),
            pl.BlockSpec((None, kdim, tn), lambda i, j: (layer, 0, j)),
            pl.BlockSpec((None, kdim, tn), lambda i, j: (layer, 0, j + nj)),
        ],
        out_specs=pl.BlockSpec((tm, tn), lambda i, j: (i, j)),
        out_shape=jax.ShapeDtypeStruct((m, half), BF16),
        compiler_params=_params(("parallel", "parallel")),
    )(a, w, w)


def _gelu(x):
    return 0.5 * x * (1.0 + lax.erf(x * (2.0 ** -0.5)))


def _layer_norm(x, g, b):
    xc = x - jnp.mean(x, axis=-1, keepdims=True)
    y = xc * lax.rsqrt(jnp.mean(xc * xc, axis=-1, keepdims=True) + LN_EPS)
    return y * g + b


def _silu(x):
    return x * jax.nn.sigmoid(x)


def _mixa_prompt_kernel(val_ref, gate_ref, w_ref, cb_ref, g_ref, b_ref, tail_ref, a_ref, st_ref, ext_ref,
                        *, batch, seq, kw, hist_pad, chunk):
    b = pl.program_id(1)

    @pl.when(b == batch)
    def _():
        a_ref[0:tail_ref.shape[0], :] = tail_ref[...].astype(a_ref.dtype)

    @pl.when(b < batch)
    def _():
        ext_ref[0:hist_pad, :] = jnp.zeros((hist_pad, LANES), F32)
        ext_ref[hist_pad:hist_pad + seq, :] = val_ref[...] * jax.nn.sigmoid(gate_ref[...])
        st_ref[...] = ext_ref[hist_pad + seq - (kw - 1):hist_pad + seq, :]
        off = hist_pad - (kw - 1)
        for c in range(seq // chunk):
            t0 = c * chunk
            acc = w_ref[0:1, :] * ext_ref[t0 + off:t0 + off + chunk, :]
            for k in range(1, kw):
                acc = acc + w_ref[k:k + 1, :] * ext_ref[t0 + off + k:t0 + off + k + chunk, :]
            y = _layer_norm(acc + cb_ref[...], g_ref[...], b_ref[...])
            a_ref[t0:t0 + chunk, :] = _silu(y).astype(a_ref.dtype)


def _mixa_prompt(z, tail, conv_w, conv_b, norm_g, norm_b, layer, *, batch, seq, width):
    kw = conv_w.shape[1]
    nc = width // LANES
    hist_pad = 32
    assert tail.shape[0] == z.shape[0] - batch * seq and tail.shape[0] <= seq
    vec = lambda: pl.BlockSpec((None, 1, LANES), lambda c, b: (layer, 0, c))
    return pl.pallas_call(
        functools.partial(_mixa_prompt_kernel, batch=batch, seq=seq, kw=kw, hist_pad=hist_pad, chunk=128),
        grid=(nc, batch + 1),
        in_specs=[
            pl.BlockSpec((seq, LANES), lambda c, b: (b, c)),
            pl.BlockSpec((seq, LANES), lambda c, b: (b, c + nc)),
            pl.BlockSpec((None, kw, LANES), lambda c, b: (layer, 0, c)),
            vec(), vec(), vec(),
            pl.BlockSpec((tail.shape[0], LANES), lambda c, b: (0, c)),
        ],
        out_specs=[
            pl.BlockSpec((seq, LANES), lambda c, b: (b, c)),
            pl.BlockSpec((None, kw - 1, LANES), lambda c, b: (jnp.minimum(b, batch - 1), 0, c)),
        ],
        out_shape=[
            jax.ShapeDtypeStruct((z.shape[0], width), BF16),
            jax.ShapeDtypeStruct((batch, kw - 1, width), F32),
        ],
        scratch_shapes=[pltpu.VMEM((hist_pad + seq, LANES), F32)],
        compiler_params=_params(("parallel", "arbitrary")),
    )(z, z, conv_w, conv_b, norm_g, norm_b, tail)


def _mixb_prompt_kernel(u_ref, v_ref, ng_ref, nb_ref, ws_ref, bst_ref, tail_ref, o_ref, *, groups, nchunks):
    step = pl.program_id(0)

    @pl.when(step == nchunks)
    def _():
        o_ref[...] = tail_ref[...].astype(o_ref.dtype)

    @pl.when(step < nchunks)
    def _():
        v = _layer_norm(_gelu(v_ref[...]), ng_ref[...], nb_ref[...]).astype(BF16)
        n = ws_ref.shape[-1]
        causal = lax.broadcasted_iota(jnp.int32, (n, n), 1) <= lax.broadcasted_iota(jnp.int32, (n, n), 0)
        for g in range(groups):
            cols = slice(g * LANES, (g + 1) * LANES)
            w_m = jnp.where(causal, ws_ref[g], 0.0).astype(BF16)
            s = jnp.dot(w_m, v[:, cols], preferred_element_type=F32) + bst_ref[:, g:g + 1]
            o_ref[:, cols] = (_gelu(u_ref[:, cols]) * s).astype(o_ref.dtype)


def _mixb_prompt(z, tail, norm_g, norm_b, spatial_w, spatial_bt, layer, *, rows, width):
    groups, chunk = spatial_w.shape[1], spatial_w.shape[2]
    assert tail.shape[0] == z.shape[0] - rows == chunk
    vec = lambda: pl.BlockSpec((None, 1, width), lambda s: (layer, 0, 0))
    return pl.pallas_call(
        functools.partial(_mixb_prompt_kernel, groups=groups, nchunks=rows // chunk),
        grid=(rows // chunk + 1,),
        in_specs=[
            pl.BlockSpec((chunk, width), lambda s: (s, 2)),
            pl.BlockSpec((chunk, width), lambda s: (s, 3)),
            vec(), vec(),
            pl.BlockSpec((None, groups, chunk, chunk), lambda s: (layer, 0, 0, 0)),
            pl.BlockSpec((None, chunk, groups), lambda s: (layer, 0, 0)),
            pl.BlockSpec((chunk, width), lambda s: (0, 1)),
        ],
        out_specs=pl.BlockSpec((chunk, width), lambda s: (s, 0)),
        out_shape=jax.ShapeDtypeStruct((z.shape[0], width), BF16),
        compiler_params=_params(("parallel",)),
    )(z, z, norm_g, norm_b, spatial_w, spatial_bt, tail)


def _mix_sample_kernel(val_ref, gate_ref, u_ref, v_ref, st_ref, cw_ref, cb_ref, ag_ref, ab_ref,
                       bg_ref, bb_ref, sw_ref, sb_ref, y_ref, ain_ref, vout_ref, *, kw, groups, width):
    n = val_ref.shape[0]
    y_ref[...] = jnp.zeros(y_ref.shape, y_ref.dtype)
    a_in = val_ref[...] * jax.nn.sigmoid(gate_ref[...])
    ain_ref[...] = a_in
    acc = cw_ref[0:1, :] * st_ref[0]
    for k in range(1, kw - 1):
        acc = acc + cw_ref[k:k + 1, :] * st_ref[k]
    acc = acc + cw_ref[kw - 1:kw, :] * a_in + cb_ref[...]
    for g in range(groups):
        cols = slice(g * LANES, (g + 1) * LANES)
        y = _layer_norm(acc[:, cols], ag_ref[:, cols], ab_ref[:, cols])
        y_ref[0:n, cols] = _silu(y)
    v = _layer_norm(_gelu(v_ref[...]), bg_ref[...], bb_ref[...])
    vout_ref[...] = v
    y_ref[0:n, width:2 * width] = _gelu(u_ref[...]) * (sw_ref[...] * v + sb_ref[...])


def _mix_sample(z, state_t, conv_w, conv_b, a_g, a_b, b_g, b_b, s_w0, s_b0, layer, *, row0, n, width):
    tail = z.shape[0] - row0
    kw = conv_w.shape[1]
    rb = row0 // n
    zspec = lambda c: pl.BlockSpec((n, width), lambda i: (rb, c))
    vec = lambda: pl.BlockSpec((None, 1, width), lambda i: (layer, 0, 0))
    return pl.pallas_call(
        functools.partial(_mix_sample_kernel, kw=kw, groups=A_GROUPS, width=width),
        grid=(1,),
        in_specs=[
            zspec(0), zspec(1), zspec(2), zspec(3),
            pl.BlockSpec((kw - 1, n, width), lambda i: (0, 0, 0)),
            pl.BlockSpec((None, kw, width), lambda i: (layer, 0, 0)),
            vec(), vec(), vec(), vec(), vec(), vec(), vec(),
        ],
        out_specs=[
            pl.BlockSpec((tail, 2 * width), lambda i: (0, 0)),
            pl.BlockSpec((n, width), lambda i: (0, 0)),
            pl.BlockSpec((n, width), lambda i: (0, 0)),
        ],
        out_shape=[
            jax.ShapeDtypeStruct((tail, 2 * width), F32),
            jax.ShapeDtypeStruct((n, width), F32),
            jax.ShapeDtypeStruct((n, width), F32),
        ],
        compiler_params=_params(("arbitrary",)),
    )(z, z, z, z, state_t, conv_w, conv_b, a_g, a_b, b_g, b_b, s_w0, s_b0)


def _rows(start, size, stride):
    return pl.ds(start, size) if stride == 1 else pl.ds(start, size, stride=stride)


LOG2E = 1.4426950408889634


def _attn_prompt_kernel(q_ref, k_ref, v_ref, tail_ref, o_ref, bias1_ref, bias2_ref, *stats,
                        batch, patterns, blk, scale):
    @pl.when(pl.program_id(0) == batch)
    def _():
        o_ref[0:tail_ref.shape[0], :] = tail_ref[...].astype(o_ref.dtype)

    @pl.when(pl.program_id(0) < batch)
    def _():
        _attn_prompt_sequence(q_ref, k_ref, v_ref, o_ref, bias1_ref, bias2_ref, stats, patterns, blk, scale)


def _attn_prompt_sequence(q_ref, k_ref, v_ref, o_ref, bias1_ref, bias2_ref, stats, patterns, blk, scale):
    seq, e = q_ref.shape
    npat = len(patterns)
    acc_refs, m_refs, l_refs = stats[:npat], stats[npat:2 * npat], stats[2 * npat:]
    c = scale * LOG2E

    def distance(nk):
        qi = lax.broadcasted_iota(jnp.int32, (blk, nk), 0)
        kj = lax.broadcasted_iota(jnp.int32, (blk, nk), 1)
        return (nk - blk) + qi - kj

    def attend(q_rows, k_rows, bias):
        qb = q_ref[q_rows, :].astype(BF16)
        kb = k_ref[k_rows, :].astype(BF16)
        vb = v_ref[k_rows, :].astype(BF16)
        s = lax.dot_general(qb, kb, (((1,), (1,)), ((), ())), preferred_element_type=F32) + bias
        m = jnp.max(s, axis=-1, keepdims=True)
        p = jnp.exp2((s - m) * c).astype(BF16)
        ov = jnp.dot(p, jnp.concatenate([vb, jnp.ones_like(vb)], axis=1), preferred_element_type=F32)
        return ov[:, :e], ov[:, e:], jnp.broadcast_to(m, (blk, e))

    for p, (window, dil) in enumerate(patterns):
        reach = window // dil
        nblk = seq // dil // blk
        for bias_ref in (bias1_ref, bias2_ref):
            dist = distance(bias_ref.shape[1])
            bias_ref[...] = jnp.where((dist >= 0) & (dist <= reach), 0.0, -jnp.inf)
        for r in range(dil):
            for b in range(nblk):
                q_rows = _rows(r + dil * blk * b, blk, dil)
                if b == 0:
                    k_rows, bias = q_rows, bias1_ref[...]
                else:
                    k_rows, bias = _rows(r + dil * blk * (b - 1), 2 * blk, dil), bias2_ref[...]
                acc_refs[p][q_rows, :], l_refs[p][q_rows, :], m_refs[p][q_rows, :] = attend(q_rows, k_rows, bias)

    def merge(t, carry):
        rows = pl.ds(pl.multiple_of(t * blk, blk), blk)
        ms = [m_ref[rows, :] for m_ref in m_refs]
        top = functools.reduce(jnp.maximum, ms)
        ws = [jnp.exp2((m - top) * c) for m in ms]
        den = functools.reduce(lambda x, y: x + y, [w * l_ref[rows, :] for w, l_ref in zip(ws, l_refs)])
        num = functools.reduce(lambda x, y: x + y, [w * a_ref[rows, :] for w, a_ref in zip(ws, acc_refs)])
        o_ref[rows, :] = (num / den).astype(o_ref.dtype)
        return carry

    lax.fori_loop(0, seq // blk, merge, 0)


def _attn_prompt(q, k, v, tail, *, batch, seq, heads):
    e = q.shape[1] // heads
    assert e == LANES and tail.shape[0] == q.shape[0] - batch * seq and tail.shape[0] <= seq
    for window, dil in C_PATTERNS:
        assert window // dil <= C_BLOCK and (seq // dil) % C_BLOCK == 0
    spec = pl.BlockSpec((seq, e), lambda b, h: (b, h))
    nstats = 3 * len(C_PATTERNS)
    return pl.pallas_call(
        functools.partial(_attn_prompt_kernel, batch=batch, patterns=C_PATTERNS, blk=C_BLOCK, scale=e ** -0.5),
        grid=(batch + 1, heads),
        in_specs=[spec, spec, spec, pl.BlockSpec((tail.shape[0], e), lambda b, h: (0, h))],
        out_specs=spec,
        out_shape=jax.ShapeDtypeStruct(q.shape, BF16),
        scratch_shapes=(
            [pltpu.VMEM((C_BLOCK, C_BLOCK), F32), pltpu.VMEM((C_BLOCK, 2 * C_BLOCK), F32)]
            + [pltpu.VMEM((seq, e), F32) for _ in range(nstats)]
        ),
        compiler_params=_params(("parallel", "parallel")),
    )(q, k, v, tail)


def _attn_sample_kernel(q_ref, kn_ref, vn_ref, *refs, npat, scale):
    kc_refs, vc_refs, o_ref = refs[:npat], refs[npat:2 * npat], refs[2 * npat]
    q = q_ref[...]
    s_new = jnp.sum(q * kn_ref[...], axis=-1, keepdims=True) * scale
    scores = [jnp.sum(kc[...] * q[None], axis=-1, keepdims=True) * scale for kc in kc_refs]
    top = s_new
    for s in scores:
        top = jnp.maximum(top, jnp.max(s, axis=0))
    w_new = npat * jnp.exp(s_new - top)
    den = w_new
    num = w_new * vn_ref[...]
    for s, vc in zip(scores, vc_refs):
        p = jnp.exp(s - top[None])
        den = den + jnp.sum(p, axis=0)
        num = num + jnp.sum(p * vc[...], axis=0)
    o_ref[...] = num / den


def _attn_sample(q_s, k_s, v_s, cache_k, cache_v, layer, *, head_group=8):
    n, heads, e = q_s.shape
    hist = cache_k.shape[2]
    new = pl.BlockSpec((None, head_group, e), lambda i, g: (i, g, 0))
    views, specs = [], []
    for window, dil in C_PATTERNS:
        reach = window // dil
        assert window <= hist and hist % dil == 0 and (hist // dil) % reach == 0
        last = hist // dil // reach - 1
        views.append((cache_k.shape[0], n, hist // dil, dil, heads, e))
        specs.append(pl.BlockSpec((None, None, reach, None, head_group, e),
                                  lambda i, g, last=last: (layer, i, last, 0, g, 0)))
    return pl.pallas_call(
        functools.partial(_attn_sample_kernel, npat=len(C_PATTERNS), scale=e ** -0.5),
        grid=(n, heads // head_group),
        in_specs=[new, new, new] + specs + specs,
        out_specs=new,
        out_shape=jax.ShapeDtypeStruct((n, heads, e), F32),
        compiler_params=_params(("parallel", "parallel")),
    )(q_s, k_s, v_s, *[cache_k.reshape(v) for v in views], *[cache_v.reshape(v) for v in views])


def _cache_shift_kernel(ck_ref, cv_ref, kn_ref, vn_ref, ok_ref, ov_ref, sems, *, layer):
    n, hist = ok_ref.shape[0], ok_ref.shape[1]
    copies = []
    for c, (old, new, out) in enumerate(((ck_ref, kn_ref, ok_ref), (cv_ref, vn_ref, ov_ref))):
        for i in range(n):
            copies.append(pltpu.make_async_copy(
                old.at[layer, i, pl.ds(1, hist - 1)], out.at[i, pl.ds(0, hist - 1)], sems.at[c, i]))
        copies.append(pltpu.make_async_copy(new, out.at[:, pl.ds(hist - 1, 1)], sems.at[c, n]))
    for cp in copies:
        cp.start()
    for cp in copies:
        cp.wait()


def _cache_shift(cache_k, cache_v, k_new, v_new, layer):
    n = cache_k.shape[1]
    any_spec = pl.BlockSpec(memory_space=pl.ANY)
    out = jax.ShapeDtypeStruct(cache_k.shape[1:], cache_k.dtype)
    return pl.pallas_call(
        functools.partial(_cache_shift_kernel, layer=layer),
        in_specs=[any_spec] * 4,
        out_specs=[any_spec] * 2,
        out_shape=[out, out],
        scratch_shapes=[pltpu.SemaphoreType.DMA((2, n + 1))],
    )(cache_k, cache_v, k_new, v_new)


def kernel(x_prompt, x_sample, state_a_conv, cache_c_k, cache_c_v, ln_mix_even, w_in_even, a_conv_w,
           a_conv_b, a_norm_g, a_norm_b, b_norm_g, b_norm_b, b_spatial_w, b_spatial_b, w_out_even,
           ln_mix_odd, w_qkv_odd, w_o_odd, ln_ffn, w_gate_up, w_down, ln_final):
    batch, seq, d = x_prompt.shape
    nsamp = x_sample.shape[0] * x_sample.shape[1]
    depth = ln_ffn.shape[0]
    np_rows = batch * seq
    rows_all = pl.cdiv(np_rows + nsamp, ROW_TILE) * ROW_TILE
    tail = rows_all - np_rows
    assert rows_all % RMS_TILE == 0 and np_rows % RMS_TILE_OUT == 0 and np_rows % nsamp == 0
    width = a_conv_w.shape[-1]
    ffn = w_down.shape[1]
    e = d // C_HEADS

    def tail_rows(s):
        return jnp.concatenate([s, jnp.zeros((tail - nsamp, s.shape[1]), s.dtype)], axis=0)

    vec3 = lambda t: t.reshape(t.shape[0], 1, t.shape[-1])
    x = jnp.concatenate([x_prompt.reshape(np_rows, d), tail_rows(x_sample.reshape(nsamp, d))], axis=0)
    rms = functools.partial(_rmsnorm, out_dtype=BF16, rows=rows_all, row_tile=RMS_TILE)

    a_hist_p, a_hist_s, b_v_s = [], [], []
    c_k_p, c_v_p, c_k_s, c_v_s = [], [], [], []
    for i in range(depth):
        j = i // 2
        if i % 2 == 0:
            h = rms(x, vec3(ln_mix_even), j)
            z = _matmul([h], w_in_even, j, n=w_in_even.shape[-1])
            s_w0 = jnp.repeat(b_spatial_w[:, :, 0, 0], width // B_GROUPS, axis=-1)
            s_b0 = jnp.repeat(b_spatial_b[:, :, 0], width // B_GROUPS, axis=-1)
            y_s, a_in_s, v_s = _mix_sample(
                z, jnp.swapaxes(state_a_conv[j], 0, 1), a_conv_w, vec3(a_conv_b), vec3(a_norm_g),
                vec3(a_norm_b), vec3(b_norm_g), vec3(b_norm_b), vec3(s_w0), vec3(s_b0), j,
                row0=np_rows, n=nsamp, width=width)
            a, hist_p = _mixa_prompt(z, y_s, a_conv_w, vec3(a_conv_b), vec3(a_norm_g), vec3(a_norm_b), j,
                                     batch=batch, seq=seq, width=width)
            b = _mixb_prompt(z, y_s, vec3(b_norm_g), vec3(b_norm_b), b_spatial_w,
                             jnp.swapaxes(b_spatial_b, 1, 2), j, rows=np_rows, width=width)
            x = _matmul([a, b], w_out_even, j, n=d, res=x)
            a_hist_p.append(hist_p)
            a_hist_s.append(jnp.concatenate([state_a_conv[j][:, 1:], a_in_s[:, None, :]], axis=1))
            b_v_s.append(v_s.reshape(x_sample.shape[0], x_sample.shape[1], width))
        else:
            h = rms(x, vec3(ln_mix_odd), j)
            q, k, v = [_matmul([h], w_qkv_odd, j, n=d, n_off=c * d) for c in range(3)]
            q_s, k_s, v_s = [t[np_rows:np_rows + nsamp].reshape(nsamp, C_HEADS, e) for t in (q, k, v)]
            o_s = _attn_sample(q_s, k_s, v_s, cache_c_k, cache_c_v, j)
            o = _attn_prompt(q, k, v, tail_rows(o_s.reshape(nsamp, d)), batch=batch, seq=seq, heads=C_HEADS)
            x = _matmul([o], w_o_odd, j, n=d, res=x)
            keep = min(C_PATTERNS[-1][0], seq)
            c_k_p.append(k[:np_rows].reshape(batch, seq, C_HEADS, e)[:, seq - keep:])
            c_v_p.append(v[:np_rows].reshape(batch, seq, C_HEADS, e)[:, seq - keep:])
            ck_s, cv_s = _cache_shift(cache_c_k, cache_c_v, k_s[:, None], v_s[:, None], j)
            c_k_s.append(ck_s)
            c_v_s.append(cv_s)
        h = rms(x, vec3(ln_ffn), i)
        hid = _matmul_gated(h, w_gate_up, i, half=ffn)
        x = _matmul([hid], w_down, i, n=d, tn=GATE_TILE, nk=2, res=x)

    g_final = ln_final.reshape(1, 1, d)
    y_prompt = _rmsnorm(x, g_final, 0, F32, rows=np_rows, row_tile=RMS_TILE_OUT)
    y_sample = _rmsnorm(x, g_final, 0, F32, rows=nsamp, row_tile=nsamp, first_block=np_rows // nsamp)
    return (y_prompt.reshape(batch, seq, d), y_sample.reshape(x_sample.shape),
            jnp.stack(a_hist_p), jnp.stack(a_hist_s), jnp.stack(b_v_s),
            jnp.stack(c_k_p), jnp.stack(c_v_p), jnp.stack(c_k_s), jnp.stack(c_v_s))
```

```python
import functools

import jax
import jax.numpy as jnp
from jax import lax
from jax.experimental import pallas as pl
from jax.experimental.pallas import tpu as pltpu

F32 = jnp.float32
BF16 = jnp.bfloat16

LANES = 128
VMEM_BYTES = 64 * 1024 * 1024
VMEM_LIMIT = VMEM_BYTES - 8 * 1024 * 1024

RMS_EPS = 1e-6
LN_EPS = 1e-5

A_GROUPS = 16
B_GROUPS = 16
B_CHUNK = 128
C_HEADS = 32
C_PATTERNS = ((128, 1), (512, 4), (2048, 16))
C_BLOCK = 128

ROW_TILE = 2080
DOWN_ROW_TILE = 1040
N_TILE = 512
GATE_TILE = 256
RMS_TILE = 320
RMS_TILE_OUT = 256


def _params(semantics):
    return pltpu.CompilerParams(dimension_semantics=semantics, vmem_limit_bytes=VMEM_LIMIT)


def _rms_kernel(x_ref, g_ref, o_ref):
    x = x_ref[...]
    y = x * lax.rsqrt(jnp.mean(x * x, axis=-1, keepdims=True) + RMS_EPS)
    o_ref[...] = (y * g_ref[...]).astype(o_ref.dtype)


def _rmsnorm(x, gains, layer, out_dtype, *, rows, row_tile, first_block=0):
    d = x.shape[1]
    return pl.pallas_call(
        _rms_kernel,
        grid=(rows // row_tile,),
        in_specs=[
            pl.BlockSpec((row_tile, d), lambda i: (i + first_block, 0)),
            pl.BlockSpec((None, 1, d), lambda i: (layer, 0, 0)),
        ],
        out_specs=pl.BlockSpec((row_tile, d), lambda i: (i, 0)),
        out_shape=jax.ShapeDtypeStruct((rows, d), out_dtype),
        compiler_params=_params(("parallel",)),
    )(x, gains)


def _mm_kernel(*refs, a_widths, nk, tk, has_res):
    na = len(a_widths)
    a_refs, b_ref = refs[:na], refs[na]
    res_ref = refs[na + 1] if has_res else None
    o_ref = refs[na + 2] if has_res else refs[na + 1]

    def finish(acc):
        if has_res:
            acc = acc + res_ref[...]
        o_ref[...] = acc.astype(o_ref.dtype)

    if nk == 1:
        acc, k0 = None, 0
        for a_ref, kw in zip(a_refs, a_widths):
            part = jnp.dot(a_ref[...], b_ref[k0:k0 + kw, :].astype(BF16), preferred_element_type=F32)
            acc = part if acc is None else acc + part
            k0 += kw
        finish(acc)
        return

    (a_ref,), acc_ref = a_refs, refs[-1]
    k = pl.program_id(2)
    for kk in range(nk):
        @pl.when(k == kk)
        def _(kk=kk):
            part = jnp.dot(a_ref[:, kk * tk:(kk + 1) * tk], b_ref[...].astype(BF16),
                           preferred_element_type=F32)
            if kk == 0:
                acc_ref[...] = part
            elif kk < nk - 1:
                acc_ref[...] += part
            else:
                finish(acc_ref[...] + part)


def _matmul(a_parts, w, layer, *, n, n_off=0, tm=ROW_TILE, tn=N_TILE, nk=1, res=None, out_dtype=F32):
    m = a_parts[0].shape[0]
    a_widths = tuple(a.shape[1] for a in a_parts)
    kdim = sum(a_widths)
    assert nk == 1 or len(a_parts) == 1
    tk = kdim // nk
    joff = n_off // tn
    has_res = res is not None
    in_specs = [pl.BlockSpec((tm, kw), lambda i, j, k: (i, 0), pipeline_mode=pl.Buffered(1)) for kw in a_widths]
    in_specs.append(pl.BlockSpec((None, tk, tn), lambda i, j, k: (layer, k, j + joff)))
    args = [*a_parts, w]
    if has_res:
        in_specs.append(pl.BlockSpec((tm, tn), lambda i, j, k: (i, j)))
        args.append(res)
    scratch = [pltpu.VMEM((tm, tn), F32)] if nk > 1 else []
    return pl.pallas_call(
        functools.partial(_mm_kernel, a_widths=a_widths, nk=nk, tk=tk, has_res=has_res),
        grid=(m // tm, n // tn, nk),
        in_specs=in_specs,
        out_specs=pl.BlockSpec((tm, tn), lambda i, j, k: (i, j)),
        out_shape=jax.ShapeDtypeStruct((m, n), out_dtype),
        scratch_shapes=scratch,
        compiler_params=_params(("parallel", "parallel", "arbitrary")),
    )(*args)


def _mm_gated_kernel(a_ref, bg_ref, bu_ref, o_ref):
    a = a_ref[...]
    g = jnp.dot(a, bg_ref[...].astype(BF16), preferred_element_type=F32)
    u = jnp.dot(a, bu_ref[...].astype(BF16), preferred_element_type=F32)
    o_ref[...] = (g * jax.nn.sigmoid(g) * u).astype(o_ref.dtype)


def _matmul_gated(a, w, layer, *, half, tm=ROW_TILE, tn=GATE_TILE):
    m, kdim = a.shape
    nj = half // tn
    return pl.pallas_call(
        _mm_gated_kernel,
        grid=(m // tm, nj),
        in_specs=[
            pl.BlockSpec((tm, kdim), lambda i, j: (i, 0), pipeline_mode=pl.Buffered(1)),
            pl.BlockSpec((None, kdim, tn), lambda i, j: (layer, 0, j)),
            pl.BlockSpec((None, kdim, tn), lambda i, j: (layer, 0, j + nj)),
        ],
        out_specs=pl.BlockSpec((tm, tn), lambda i, j: (i, j)),
        out_shape=jax.ShapeDtypeStruct((m, half), BF16),
        compiler_params=_params(("parallel", "parallel")),
    )(a, w, w)


def _gelu(x):
    return 0.5 * x * (1.0 + lax.erf(x * (2.0 ** -0.5)))


def _layer_norm(x, g, b):
    xc = x - jnp.mean(x, axis=-1, keepdims=True)
    y = xc * lax.rsqrt(jnp.mean(xc * xc, axis=-1, keepdims=True) + LN_EPS)
    return y * g + b


def _silu(x):
    return x * jax.nn.sigmoid(x)


def _mixa_prompt_kernel(val_ref, gate_ref, w_ref, cb_ref, g_ref, b_ref, tail_ref, a_ref, st_ref, ext_ref,
                        *, batch, seq, kw, hist_pad, chunk):
    b = pl.program_id(1)

    @pl.when(b == batch)
    def _():
        a_ref[0:tail_ref.shape[0], :] = tail_ref[...].astype(a_ref.dtype)

    @pl.when(b < batch)
    def _():
        ext_ref[0:hist_pad, :] = jnp.zeros((hist_pad, LANES), F32)
        ext_ref[hist_pad:hist_pad + seq, :] = val_ref[...] * jax.nn.sigmoid(gate_ref[...])
        st_ref[...] = ext_ref[hist_pad + seq - (kw - 1):hist_pad + seq, :]
        off = hist_pad - (kw - 1)
        for c in range(seq // chunk):
            t0 = c * chunk
            acc = w_ref[0:1, :] * ext_ref[t0 + off:t0 + off + chunk, :]
            for k in range(1, kw):
                acc = acc + w_ref[k:k + 1, :] * ext_ref[t0 + off + k:t0 + off + k + chunk, :]
            y = _layer_norm(acc + cb_ref[...], g_ref[...], b_ref[...])
            a_ref[t0:t0 + chunk, :] = _silu(y).astype(a_ref.dtype)


def _mixa_prompt(z, tail, conv_w, conv_b, norm_g, norm_b, layer, *, batch, seq, width):
    kw = conv_w.shape[1]
    nc = width // LANES
    hist_pad = 32
    assert tail.shape[0] == z.shape[0] - batch * seq and tail.shape[0] <= seq
    vec = lambda: pl.BlockSpec((None, 1, LANES), lambda c, b: (layer, 0, c))
    return pl.pallas_call(
        functools.partial(_mixa_prompt_kernel, batch=batch, seq=seq, kw=kw, hist_pad=hist_pad, chunk=128),
        grid=(nc, batch + 1),
        in_specs=[
            pl.BlockSpec((seq, LANES), lambda c, b: (b, c)),
            pl.BlockSpec((seq, LANES), lambda c, b: (b, c + nc)),
            pl.BlockSpec((None, kw, LANES), lambda c, b: (layer, 0, c)),
            vec(), vec(), vec(),
            pl.BlockSpec((tail.shape[0], LANES), lambda c, b: (0, c)),
        ],
        out_specs=[
            pl.BlockSpec((seq, LANES), lambda c, b: (b, c)),
            pl.BlockSpec((None, kw - 1, LANES), lambda c, b: (jnp.minimum(b, batch - 1), 0, c)),
        ],
        out_shape=[
            jax.ShapeDtypeStruct((z.shape[0], width), BF16),
            jax.ShapeDtypeStruct((batch, kw - 1, width), F32),
        ],
        scratch_shapes=[pltpu.VMEM((hist_pad + seq, LANES), F32)],
        compiler_params=_params(("parallel", "arbitrary")),
    )(z, z, conv_w, conv_b, norm_g, norm_b, tail)


def _mixb_prompt_kernel(u_ref, v_ref, ng_ref, nb_ref, ws_ref, bst_ref, tail_ref, o_ref, *, groups, nchunks):
    step = pl.program_id(0)

    @pl.when(step == nchunks)
    def _():
        o_ref[...] = tail_ref[...].astype(o_ref.dtype)

    @pl.when(step < nchunks)
    def _():
        v = _layer_norm(_gelu(v_ref[...]), ng_ref[...], nb_ref[...]).astype(BF16)
        n = ws_ref.shape[-1]
        causal = lax.broadcasted_iota(jnp.int32, (n, n), 1) <= lax.broadcasted_iota(jnp.int32, (n, n), 0)
        for g in range(groups):
            cols = slice(g * LANES, (g + 1) * LANES)
            w_m = jnp.where(causal, ws_ref[g], 0.0).astype(BF16)
            s = jnp.dot(w_m, v[:, cols], preferred_element_type=F32) + bst_ref[:, g:g + 1]
            o_ref[:, cols] = (_gelu(u_ref[:, cols]) * s).astype(o_ref.dtype)


def _mixb_prompt(z, tail, norm_g, norm_b, spatial_w, spatial_bt, layer, *, rows, width):
    groups, chunk = spatial_w.shape[1], spatial_w.shape[2]
    assert tail.shape[0] == z.shape[0] - rows == chunk
    vec = lambda: pl.BlockSpec((None, 1, width), lambda s: (layer, 0, 0))
    return pl.pallas_call(
        functools.partial(_mixb_prompt_kernel, groups=groups, nchunks=rows // chunk),
        grid=(rows // chunk + 1,),
        in_specs=[
            pl.BlockSpec((chunk, width), lambda s: (s, 2)),
            pl.BlockSpec((chunk, width), lambda s: (s, 3)),
            vec(), vec(),
            pl.BlockSpec((None, groups, chunk, chunk), lambda s: (layer, 0, 0, 0)),
            pl.BlockSpec((None, chunk, groups), lambda s: (layer, 0, 0)),
            pl.BlockSpec((chunk, width), lambda s: (0, 1)),
        ],
        out_specs=pl.BlockSpec((chunk, width), lambda s: (s, 0)),
        out_shape=jax.ShapeDtypeStruct((z.shape[0], width), BF16),
        compiler_params=_params(("parallel",)),
    )(z, z, norm_g, norm_b, spatial_w, spatial_bt, tail)


def _mix_sample_kernel(val_ref, gate_ref, u_ref, v_ref, st_ref, cw_ref, cb_ref, ag_ref, ab_ref,
                       bg_ref, bb_ref, sw_ref, sb_ref, y_ref, ain_ref, vout_ref, *, kw, groups, width):
    n = val_ref.shape[0]
    y_ref[...] = jnp.zeros(y_ref.shape, y_ref.dtype)
    a_in = val_ref[...] * jax.nn.sigmoid(gate_ref[...])
    ain_ref[...] = a_in
    acc = cw_ref[0:1, :] * st_ref[0]
    for k in range(1, kw - 1):
        acc = acc + cw_ref[k:k + 1, :] * st_ref[k]
    acc = acc + cw_ref[kw - 1:kw, :] * a_in + cb_ref[...]
    for g in range(groups):
        cols = slice(g * LANES, (g + 1) * LANES)
        y = _layer_norm(acc[:, cols], ag_ref[:, cols], ab_ref[:, cols])
        y_ref[0:n, cols] = _silu(y)
    v = _layer_norm(_gelu(v_ref[...]), bg_ref[...], bb_ref[...])
    vout_ref[...] = v
    y_ref[0:n, width:2 * width] = _gelu(u_ref[...]) * (sw_ref[...] * v + sb_ref[...])


def _mix_sample(z, state_t, conv_w, conv_b, a_g, a_b, b_g, b_b, s_w0, s_b0, layer, *, row0, n, width):
    tail = z.shape[0] - row0
    kw = conv_w.shape[1]
    rb = row0 // n
    zspec = lambda c: pl.BlockSpec((n, width), lambda i: (rb, c))
    vec = lambda: pl.BlockSpec((None, 1, width), lambda i: (layer, 0, 0))
    return pl.pallas_call(
        functools.partial(_mix_sample_kernel, kw=kw, groups=A_GROUPS, width=width),
        grid=(1,),
        in_specs=[
            zspec(0), zspec(1), zspec(2), zspec(3),
            pl.BlockSpec((kw - 1, n, width), lambda i: (0, 0, 0)),
            pl.BlockSpec((None, kw, width), lambda i: (layer, 0, 0)),
            vec(), vec(), vec(), vec(), vec(), vec(), vec(),
        ],
        out_specs=[
            pl.BlockSpec((tail, 2 * width), lambda i: (0, 0)),
            pl.BlockSpec((n, width), lambda i: (0, 0)),
            pl.BlockSpec((n, width), lambda i: (0, 0)),
        ],
        out_shape=[
            jax.ShapeDtypeStruct((tail, 2 * width), F32),
            jax.ShapeDtypeStruct((n, width), F32),
            jax.ShapeDtypeStruct((n, width), F32),
        ],
        compiler_params=_params(("arbitrary",)),
    )(z, z, z, z, state_t, conv_w, conv_b, a_g, a_b, b_g, b_b, s_w0, s_b0)


def _rows(start, size, stride):
    return pl.ds(start, size) if stride == 1 else pl.ds(start, size, stride=stride)


LOG2E = 1.4426950408889634


def _attn_prompt_kernel(q_ref, k_ref, v_ref, tail_ref, o_ref, bias1_ref, bias2_ref, *stats,
                        batch, patterns, blk, scale):
    @pl.when(pl.program_id(0) == batch)
    def _():
        o_ref[0:tail_ref.shape[0], :] = tail_ref[...].astype(o_ref.dtype)

    @pl.when(pl.program_id(0) < batch)
    def _():
        _attn_prompt_sequence(q_ref, k_ref, v_ref, o_ref, bias1_ref, bias2_ref, stats, patterns, blk, scale)


def _attn_prompt_sequence(q_ref, k_ref, v_ref, o_ref, bias1_ref, bias2_ref, stats, patterns, blk, scale):
    seq, e = q_ref.shape
    npat = len(patterns)
    acc_refs, m_refs, l_refs = stats[:npat], stats[npat:2 * npat], stats[2 * npat:]
    c = scale * LOG2E

    def distance(nk):
        qi = lax.broadcasted_iota(jnp.int32, (blk, nk), 0)
        kj = lax.broadcasted_iota(jnp.int32, (blk, nk), 1)
        return (nk - blk) + qi - kj

    def attend(q_rows, k_rows, bias):
        qb = q_ref[q_rows, :].astype(BF16)
        kb = k_ref[k_rows, :].astype(BF16)
        vb = v_ref[k_rows, :].astype(BF16)
        s = lax.dot_general(qb, kb, (((1,), (1,)), ((), ())), preferred_element_type=F32) + bias
        m = jnp.max(s, axis=-1, keepdims=True)
        p = jnp.exp2((s - m) * c).astype(BF16)
        ov = jnp.dot(p, jnp.concatenate([vb, jnp.ones_like(vb)], axis=1), preferred_element_type=F32)
        return ov[:, :e], ov[:, e:], jnp.broadcast_to(m, (blk, e))

    for p, (window, dil) in enumerate(patterns):
        reach = window // dil
        nblk = seq // dil // blk
        for bias_ref in (bias1_ref, bias2_ref):
            dist = distance(bias_ref.shape[1])
            bias_ref[...] = jnp.where((dist >= 0) & (dist <= reach), 0.0, -jnp.inf)
        for r in range(dil):
            for b in range(nblk):
                q_rows = _rows(r + dil * blk * b, blk, dil)
                if b == 0:
                    k_rows, bias = q_rows, bias1_ref[...]
                else:
                    k_rows, bias = _rows(r + dil * blk * (b - 1), 2 * blk, dil), bias2_ref[...]
                acc_refs[p][q_rows, :], l_refs[p][q_rows, :], m_refs[p][q_rows, :] = attend(q_rows, k_rows, bias)

    def merge(t, carry):
        rows = pl.ds(pl.multiple_of(t * blk, blk), blk)
        ms = [m_ref[rows, :] for m_ref in m_refs]
        top = functools.reduce(jnp.maximum, ms)
        ws = [jnp.exp2((m - top) * c) for m in ms]
        den = functools.reduce(lambda x, y: x + y, [w * l_ref[rows, :] for w, l_ref in zip(ws, l_refs)])
        num = functools.reduce(lambda x, y: x + y, [w * a_ref[rows, :] for w, a_ref in zip(ws, acc_refs)])
        o_ref[rows, :] = (num / den).astype(o_ref.dtype)
        return carry

    lax.fori_loop(0, seq // blk, merge, 0)


def _attn_prompt(q, k, v, tail, *, batch, seq, heads):
    e = q.shape[1] // heads
    assert e == LANES and tail.shape[0] == q.shape[0] - batch * seq and tail.shape[0] <= seq
    for window, dil in C_PATTERNS:
        assert window // dil <= C_BLOCK and (seq // dil) % C_BLOCK == 0
    spec = pl.BlockSpec((seq, e), lambda b, h: (b, h))
    nstats = 3 * len(C_PATTERNS)
    return pl.pallas_call(
        functools.partial(_attn_prompt_kernel, batch=batch, patterns=C_PATTERNS, blk=C_BLOCK, scale=e ** -0.5),
        grid=(batch + 1, heads),
        in_specs=[spec, spec, spec, pl.BlockSpec((tail.shape[0], e), lambda b, h: (0, h))],
        out_specs=spec,
        out_shape=jax.ShapeDtypeStruct(q.shape, BF16),
        scratch_shapes=(
            [pltpu.VMEM((C_BLOCK, C_BLOCK), F32), pltpu.VMEM((C_BLOCK, 2 * C_BLOCK), F32)]
            + [pltpu.VMEM((seq, e), F32) for _ in range(nstats)]
        ),
        compiler_params=_params(("parallel", "parallel")),
    )(q, k, v, tail)


def _attn_sample_kernel(q_ref, kn_ref, vn_ref, *refs, npat, scale):
    kc_refs, vc_refs, o_ref = refs[:npat], refs[npat:2 * npat], refs[2 * npat]
    q = q_ref[...]
    s_new = jnp.sum(q * kn_ref[...], axis=-1, keepdims=True) * scale
    scores = [jnp.sum(kc[...] * q[None], axis=-1, keepdims=True) * scale for kc in kc_refs]
    top = s_new
    for s in scores:
        top = jnp.maximum(top, jnp.max(s, axis=0))
    w_new = npat * jnp.exp(s_new - top)
    den = w_new
    num = w_new * vn_ref[...]
    for s, vc in zip(scores, vc_refs):
        p = jnp.exp(s - top[None])
        den = den + jnp.sum(p, axis=0)
        num = num + jnp.sum(p * vc[...], axis=0)
    o_ref[...] = num / den


def _attn_sample(q_s, k_s, v_s, cache_k, cache_v, layer, *, head_group=8):
    n, heads, e = q_s.shape
    hist = cache_k.shape[2]
    new = pl.BlockSpec((None, head_group, e), lambda i, g: (i, g, 0))
    views, specs = [], []
    for window, dil in C_PATTERNS:
        reach = window // dil
        assert window <= hist and hist % dil == 0 and (hist // dil) % reach == 0
        last = hist // dil // reach - 1
        views.append((cache_k.shape[0], n, hist // dil, dil, heads, e))
        specs.append(pl.BlockSpec((None, None, reach, None, head_group, e),
                                  lambda i, g, last=last: (layer, i, last, 0, g, 0)))
    return pl.pallas_call(
        functools.partial(_attn_sample_kernel, npat=len(C_PATTERNS), scale=e ** -0.5),
        grid=(n, heads // head_group),
        in_specs=[new, new, new] + specs + specs,
        out_specs=new,
        out_shape=jax.ShapeDtypeStruct((n, heads, e), F32),
        compiler_params=_params(("parallel", "parallel")),
    )(q_s, k_s, v_s, *[cache_k.reshape(v) for v in views], *[cache_v.reshape(v) for v in views])


def _cache_shift_kernel(ck_ref, ck_next_ref, cv_ref, cv_next_ref, kn_ref, vn_ref, ok_ref, ov_ref, *, nblk):
    last = pl.program_id(1) == nblk - 1
    rows = ok_ref.shape[0]
    for old, nxt, new, out in ((ck_ref, ck_next_ref, kn_ref, ok_ref), (cv_ref, cv_next_ref, vn_ref, ov_ref)):
        out[0:rows - 1] = old[1:rows]
        out[rows - 1] = jnp.where(last, new[0], nxt[0])


def _cache_shift(cache_k, cache_v, k_new, v_new, layer, *, rows=256):
    _, n, hist, heads, e = cache_k.shape
    assert hist % rows == 0
    nblk = hist // rows
    block = pl.BlockSpec((None, None, rows, heads, e), lambda i, t: (layer, i, t, 0, 0))
    nxt = pl.BlockSpec((None, None, 1, heads, e),
                       lambda i, t: (layer, i, jnp.minimum((t + 1) * rows, hist - 1), 0, 0))
    new = pl.BlockSpec((None, 1, heads, e), lambda i, t: (i, 0, 0, 0))
    out_block = pl.BlockSpec((None, rows, heads, e), lambda i, t: (i, t, 0, 0))
    out = jax.ShapeDtypeStruct(cache_k.shape[1:], cache_k.dtype)
    return pl.pallas_call(
        functools.partial(_cache_shift_kernel, nblk=nblk),
        grid=(n, nblk),
        in_specs=[block, nxt, block, nxt, new, new],
        out_specs=[out_block, out_block],
        out_shape=[out, out],
        compiler_params=_params(("parallel", "parallel")),
    )(cache_k, cache_k, cache_v, cache_v, k_new, v_new)


def kernel(x_prompt, x_sample, state_a_conv, cache_c_k, cache_c_v, ln_mix_even, w_in_even, a_conv_w,
           a_conv_b, a_norm_g, a_norm_b, b_norm_g, b_norm_b, b_spatial_w, b_spatial_b, w_out_even,
           ln_mix_odd, w_qkv_odd, w_o_odd, ln_ffn, w_gate_up, w_down, ln_final):
    batch, seq, d = x_prompt.shape
    nsamp = x_sample.shape[0] * x_sample.shape[1]
    depth = ln_ffn.shape[0]
    np_rows = batch * seq
    rows_all = pl.cdiv(np_rows + nsamp, ROW_TILE) * ROW_TILE
    tail = rows_all - np_rows
    assert rows_all % RMS_TILE == 0 and np_rows % RMS_TILE_OUT == 0 and np_rows % nsamp == 0
    width = a_conv_w.shape[-1]
    ffn = w_down.shape[1]
    e = d // C_HEADS

    def tail_rows(s):
        return jnp.concatenate([s, jnp.zeros((tail - nsamp, s.shape[1]), s.dtype)], axis=0)

    vec3 = lambda t: t.reshape(t.shape[0], 1, t.shape[-1])
    x = jnp.concatenate([x_prompt.reshape(np_rows, d), tail_rows(x_sample.reshape(nsamp, d))], axis=0)
    rms = functools.partial(_rmsnorm, out_dtype=BF16, rows=rows_all, row_tile=RMS_TILE)

    a_hist_p, a_hist_s, b_v_s = [], [], []
    c_k_p, c_v_p, c_k_s, c_v_s = [], [], [], []
    for i in range(depth):
        j = i // 2
        if i % 2 == 0:
            h = rms(x, vec3(ln_mix_even), j)
            z = _matmul([h], w_in_even, j, n=w_in_even.shape[-1])
            s_w0 = jnp.repeat(b_spatial_w[:, :, 0, 0], width // B_GROUPS, axis=-1)
            s_b0 = jnp.repeat(b_spatial_b[:, :, 0], width // B_GROUPS, axis=-1)
            y_s, a_in_s, v_s = _mix_sample(
                z, jnp.swapaxes(state_a_conv[j], 0, 1), a_conv_w, vec3(a_conv_b), vec3(a_norm_g),
                vec3(a_norm_b), vec3(b_norm_g), vec3(b_norm_b), vec3(s_w0), vec3(s_b0), j,
                row0=np_rows, n=nsamp, width=width)
            a, hist_p = _mixa_prompt(z, y_s, a_conv_w, vec3(a_conv_b), vec3(a_norm_g), vec3(a_norm_b), j,
                                     batch=batch, seq=seq, width=width)
            b = _mixb_prompt(z, y_s, vec3(b_norm_g), vec3(b_norm_b), b_spatial_w,
                             jnp.swapaxes(b_spatial_b, 1, 2), j, rows=np_rows, width=width)
            x = _matmul([a, b], w_out_even, j, n=d, res=x)
            a_hist_p.append(hist_p)
            a_hist_s.append(jnp.concatenate([state_a_conv[j][:, 1:], a_in_s[:, None, :]], axis=1))
            b_v_s.append(v_s.reshape(x_sample.shape[0], x_sample.shape[1], width))
        else:
            h = rms(x, vec3(ln_mix_odd), j)
            q, k, v = [_matmul([h], w_qkv_odd, j, n=d, n_off=c * d) for c in range(3)]
            q_s, k_s, v_s = [t[np_rows:np_rows + nsamp].reshape(nsamp, C_HEADS, e) for t in (q, k, v)]
            o_s = _attn_sample(q_s, k_s, v_s, cache_c_k, cache_c_v, j)
            o = _attn_prompt(q, k, v, tail_rows(o_s.reshape(nsamp, d)), batch=batch, seq=seq, heads=C_HEADS)
            x = _matmul([o], w_o_odd, j, n=d, res=x)
            keep = min(C_PATTERNS[-1][0], seq)
            c_k_p.append(k[:np_rows].reshape(batch, seq, C_HEADS, e)[:, seq - keep:])
            c_v_p.append(v[:np_rows].reshape(batch, seq, C_HEADS, e)[:, seq - keep:])
            ck_s, cv_s = _cache_shift(cache_c_k, cache_c_v, k_s[:, None], v_s[:, None], j)
            c_k_s.append(ck_s)
            c_v_s.append(cv_s)
        h = rms(x, vec3(ln_ffn), i)
        hid = _matmul_gated(h, w_gate_up, i, half=ffn)
        x = _matmul([hid], w_down, i, n=d, tm=DOWN_ROW_TILE, tn=GATE_TILE, nk=2, res=x)

    g_final = ln_final.reshape(1, 1, d)
    y_prompt = _rmsnorm(x, g_final, 0, F32, rows=np_rows, row_tile=RMS_TILE_OUT)
    y_sample = _rmsnorm(x, g_final, 0, F32, rows=nsamp, row_tile=nsamp, first_block=np_rows // nsamp)
    return (y_prompt.reshape(batch, seq, d), y_sample.reshape(x_sample.shape),
            jnp.stack(a_hist_p), jnp.stack(a_hist_s), jnp.stack(b_v_s),
            jnp.stack(c_k_p), jnp.stack(c_v_p), jnp.stack(c_k_s), jnp.stack(c_v_s))
```

```python
import functools

import jax
import jax.numpy as jnp
from jax import lax
from jax.experimental import pallas as pl
from jax.experimental.pallas import tpu as pltpu

F32 = jnp.float32
BF16 = jnp.bfloat16

LANES = 128
VMEM_BYTES = 64 * 1024 * 1024
VMEM_LIMIT = VMEM_BYTES - 8 * 1024 * 1024

RMS_EPS = 1e-6
LN_EPS = 1e-5

A_GROUPS = 16
B_GROUPS = 16
B_CHUNK = 128
C_HEADS = 32
C_PATTERNS = ((128, 1), (512, 4), (2048, 16))
C_BLOCK = 128

ROW_TILE = 2080
DOWN_ROW_TILE = 1040
N_TILE = 512
GATE_TILE = 256
RMS_TILE = 320
RMS_TILE_OUT = 256


def _params(semantics):
    return pltpu.CompilerParams(dimension_semantics=semantics, vmem_limit_bytes=VMEM_LIMIT)


def _rms_kernel(x_ref, g_ref, o_ref):
    x = x_ref[...]
    y = x * lax.rsqrt(jnp.mean(x * x, axis=-1, keepdims=True) + RMS_EPS)
    o_ref[...] = (y * g_ref[...]).astype(o_ref.dtype)


def _rmsnorm(x, gains, layer, out_dtype, *, rows, row_tile, first_block=0):
    d = x.shape[1]
    return pl.pallas_call(
        _rms_kernel,
        grid=(rows // row_tile,),
        in_specs=[
            pl.BlockSpec((row_tile, d), lambda i: (i + first_block, 0)),
            pl.BlockSpec((None, 1, d), lambda i: (layer, 0, 0)),
        ],
        out_specs=pl.BlockSpec((row_tile, d), lambda i: (i, 0)),
        out_shape=jax.ShapeDtypeStruct((rows, d), out_dtype),
        compiler_params=_params(("parallel",)),
    )(x, gains)


def _mm_kernel(*refs, a_widths, nk, tk, has_res):
    na = len(a_widths)
    a_refs, b_ref = refs[:na], refs[na]
    res_ref = refs[na + 1] if has_res else None
    o_ref = refs[na + 2] if has_res else refs[na + 1]

    def finish(acc):
        if has_res:
            acc = acc + res_ref[...]
        o_ref[...] = acc.astype(o_ref.dtype)

    if nk == 1:
        acc, k0 = None, 0
        for a_ref, kw in zip(a_refs, a_widths):
            part = jnp.dot(a_ref[...], b_ref[k0:k0 + kw, :].astype(BF16), preferred_element_type=F32)
            acc = part if acc is None else acc + part
            k0 += kw
        finish(acc)
        return

    (a_ref,), acc_ref = a_refs, refs[-1]
    k = pl.program_id(2)
    for kk in range(nk):
        @pl.when(k == kk)
        def _(kk=kk):
            part = jnp.dot(a_ref[:, kk * tk:(kk + 1) * tk], b_ref[...].astype(BF16),
                           preferred_element_type=F32)
            if kk == 0:
                acc_ref[...] = part
            elif kk < nk - 1:
                acc_ref[...] += part
            else:
                finish(acc_ref[...] + part)


def _matmul(a_parts, w, layer, *, n, n_off=0, tm=ROW_TILE, tn=N_TILE, nk=1, res=None, out_dtype=F32):
    m = a_parts[0].shape[0]
    a_widths = tuple(a.shape[1] for a in a_parts)
    kdim = sum(a_widths)
    assert nk == 1 or len(a_parts) == 1
    tk = kdim // nk
    joff = n_off // tn
    has_res = res is not None
    in_specs = [pl.BlockSpec((tm, kw), lambda i, j, k: (i, 0), pipeline_mode=pl.Buffered(1)) for kw in a_widths]
    in_specs.append(pl.BlockSpec((None, tk, tn), lambda i, j, k: (layer, k, j + joff)))
    args = [*a_parts, w]
    if has_res:
        in_specs.append(pl.BlockSpec((tm, tn), lambda i, j, k: (i, j)))
        args.append(res)
    scratch = [pltpu.VMEM((tm, tn), F32)] if nk > 1 else []
    return pl.pallas_call(
        functools.partial(_mm_kernel, a_widths=a_widths, nk=nk, tk=tk, has_res=has_res),
        grid=(m // tm, n // tn, nk),
        in_specs=in_specs,
        out_specs=pl.BlockSpec((tm, tn), lambda i, j, k: (i, j)),
        out_shape=jax.ShapeDtypeStruct((m, n), out_dtype),
        scratch_shapes=scratch,
        compiler_params=_params(("parallel", "parallel", "arbitrary")),
    )(*args)


def _mm_gated_kernel(a_ref, bg_ref, bu_ref, *refs, nj, ride_steps):
    o_ref = refs[-1] if ride_steps == 0 else refs[-2]
    a = a_ref[...]
    g = jnp.dot(a, bg_ref[...].astype(BF16), preferred_element_type=F32)
    u = jnp.dot(a, bu_ref[...].astype(BF16), preferred_element_type=F32)
    o_ref[...] = (g * jax.nn.sigmoid(g) * u).astype(o_ref.dtype)

    if ride_steps:
        old_ref, next_ref, shifted_ref = refs[0], refs[1], refs[-1]
        rows = shifted_ref.shape[0]

        @pl.when(pl.program_id(0) * nj + pl.program_id(1) < ride_steps)
        def _():
            shifted_ref[0:rows - 1] = old_ref[1:rows]
            shifted_ref[rows - 1] = next_ref[0]


def _matmul_gated(a, w, layer, *, half, tm=ROW_TILE, tn=GATE_TILE, ride=None, ride_rows=128):
    m, kdim = a.shape
    ni, nj = m // tm, half // tn
    in_specs = [
        pl.BlockSpec((tm, kdim), lambda i, j: (i, 0), pipeline_mode=pl.Buffered(1)),
        pl.BlockSpec((None, kdim, tn), lambda i, j: (layer, 0, j)),
        pl.BlockSpec((None, kdim, tn), lambda i, j: (layer, 0, j + nj)),
    ]
    args = [a, w, w]
    out_specs = [pl.BlockSpec((tm, tn), lambda i, j: (i, j))]
    out_shape = [jax.ShapeDtypeStruct((m, half), BF16)]
    ride_steps = 0
    if ride is not None:
        cache, cache_layer = ride
        _, n, hist, heads, e = cache.shape
        assert hist % ride_rows == 0
        nblk = hist // ride_rows
        ride_steps = n * nblk
        assert ride_steps <= ni * nj

        def seq_blk(i, j):
            step = jnp.minimum(i * nj + j, ride_steps - 1)
            return step // nblk, step % nblk

        def old_map(i, j):
            sq, t = seq_blk(i, j)
            return cache_layer, sq, t, 0, 0

        def next_map(i, j):
            sq, t = seq_blk(i, j)
            return cache_layer, sq, jnp.minimum((t + 1) * ride_rows, hist - 1), 0, 0

        def out_map(i, j):
            sq, t = seq_blk(i, j)
            return sq, t, 0, 0

        in_specs += [pl.BlockSpec((None, None, ride_rows, heads, e), old_map),
                     pl.BlockSpec((None, None, 1, heads, e), next_map)]
        args += [cache, cache]
        out_specs.append(pl.BlockSpec((None, ride_rows, heads, e), out_map))
        out_shape.append(jax.ShapeDtypeStruct(cache.shape[1:], cache.dtype))
    outs = pl.pallas_call(
        functools.partial(_mm_gated_kernel, nj=nj, ride_steps=ride_steps),
        grid=(ni, nj),
        in_specs=in_specs,
        out_specs=out_specs,
        out_shape=out_shape,
        compiler_params=_params(("arbitrary", "arbitrary") if ride_steps else ("parallel", "parallel")),
    )(*args)
    return outs if ride_steps else (outs[0], None)


def _gelu(x):
    return 0.5 * x * (1.0 + lax.erf(x * (2.0 ** -0.5)))


def _layer_norm(x, g, b):
    xc = x - jnp.mean(x, axis=-1, keepdims=True)
    y = xc * lax.rsqrt(jnp.mean(xc * xc, axis=-1, keepdims=True) + LN_EPS)
    return y * g + b


def _silu(x):
    return x * jax.nn.sigmoid(x)


def _mixa_prompt_kernel(val_ref, gate_ref, w_ref, cb_ref, g_ref, b_ref, tail_ref, a_ref, st_ref, ext_ref,
                        *, batch, seq, kw, hist_pad, chunk):
    b = pl.program_id(1)

    @pl.when(b == batch)
    def _():
        a_ref[0:tail_ref.shape[0], :] = tail_ref[...].astype(a_ref.dtype)

    @pl.when(b < batch)
    def _():
        ext_ref[0:hist_pad, :] = jnp.zeros((hist_pad, LANES), F32)
        ext_ref[hist_pad:hist_pad + seq, :] = val_ref[...] * jax.nn.sigmoid(gate_ref[...])
        st_ref[...] = ext_ref[hist_pad + seq - (kw - 1):hist_pad + seq, :]
        off = hist_pad - (kw - 1)
        for c in range(seq // chunk):
            t0 = c * chunk
            acc = w_ref[0:1, :] * ext_ref[t0 + off:t0 + off + chunk, :]
            for k in range(1, kw):
                acc = acc + w_ref[k:k + 1, :] * ext_ref[t0 + off + k:t0 + off + k + chunk, :]
            y = _layer_norm(acc + cb_ref[...], g_ref[...], b_ref[...])
            a_ref[t0:t0 + chunk, :] = _silu(y).astype(a_ref.dtype)


def _mixa_prompt(z, tail, conv_w, conv_b, norm_g, norm_b, layer, *, batch, seq, width):
    kw = conv_w.shape[1]
    nc = width // LANES
    hist_pad = 32
    assert tail.shape[0] == z.shape[0] - batch * seq and tail.shape[0] <= seq
    vec = lambda: pl.BlockSpec((None, 1, LANES), lambda c, b: (layer, 0, c))
    return pl.pallas_call(
        functools.partial(_mixa_prompt_kernel, batch=batch, seq=seq, kw=kw, hist_pad=hist_pad, chunk=128),
        grid=(nc, batch + 1),
        in_specs=[
            pl.BlockSpec((seq, LANES), lambda c, b: (b, c)),
            pl.BlockSpec((seq, LANES), lambda c, b: (b, c + nc)),
            pl.BlockSpec((None, kw, LANES), lambda c, b: (layer, 0, c)),
            vec(), vec(), vec(),
            pl.BlockSpec((tail.shape[0], LANES), lambda c, b: (0, c)),
        ],
        out_specs=[
            pl.BlockSpec((seq, LANES), lambda c, b: (b, c)),
            pl.BlockSpec((None, kw - 1, LANES), lambda c, b: (jnp.minimum(b, batch - 1), 0, c)),
        ],
        out_shape=[
            jax.ShapeDtypeStruct((z.shape[0], width), BF16),
            jax.ShapeDtypeStruct((batch, kw - 1, width), F32),
        ],
        scratch_shapes=[pltpu.VMEM((hist_pad + seq, LANES), F32)],
        compiler_params=_params(("parallel", "arbitrary")),
    )(z, z, conv_w, conv_b, norm_g, norm_b, tail)


def _mixb_prompt_kernel(u_ref, v_ref, ng_ref, nb_ref, ws_ref, bst_ref, tail_ref, o_ref, *, groups, nchunks):
    step = pl.program_id(0)

    @pl.when(step == nchunks)
    def _():
        o_ref[...] = tail_ref[...].astype(o_ref.dtype)

    @pl.when(step < nchunks)
    def _():
        v = _layer_norm(_gelu(v_ref[...]), ng_ref[...], nb_ref[...]).astype(BF16)
        n = ws_ref.shape[-1]
        causal = lax.broadcasted_iota(jnp.int32, (n, n), 1) <= lax.broadcasted_iota(jnp.int32, (n, n), 0)
        for g in range(groups):
            cols = slice(g * LANES, (g + 1) * LANES)
            w_m = jnp.where(causal, ws_ref[g], 0.0).astype(BF16)
            s = jnp.dot(w_m, v[:, cols], preferred_element_type=F32) + bst_ref[:, g:g + 1]
            o_ref[:, cols] = (_gelu(u_ref[:, cols]) * s).astype(o_ref.dtype)


def _mixb_prompt(z, tail, norm_g, norm_b, spatial_w, spatial_bt, layer, *, rows, width):
    groups, chunk = spatial_w.shape[1], spatial_w.shape[2]
    assert tail.shape[0] == z.shape[0] - rows == chunk
    vec = lambda: pl.BlockSpec((None, 1, width), lambda s: (layer, 0, 0))
    return pl.pallas_call(
        functools.partial(_mixb_prompt_kernel, groups=groups, nchunks=rows // chunk),
        grid=(rows // chunk + 1,),
        in_specs=[
            pl.BlockSpec((chunk, width), lambda s: (s, 2)),
            pl.BlockSpec((chunk, width), lambda s: (s, 3)),
            vec(), vec(),
            pl.BlockSpec((None, groups, chunk, chunk), lambda s: (layer, 0, 0, 0)),
            pl.BlockSpec((None, chunk, groups), lambda s: (layer, 0, 0)),
            pl.BlockSpec((chunk, width), lambda s: (0, 1)),
        ],
        out_specs=pl.BlockSpec((chunk, width), lambda s: (s, 0)),
        out_shape=jax.ShapeDtypeStruct((z.shape[0], width), BF16),
        compiler_params=_params(("parallel",)),
    )(z, z, norm_g, norm_b, spatial_w, spatial_bt, tail)


def _mix_sample_kernel(val_ref, gate_ref, u_ref, v_ref, st_ref, cw_ref, cb_ref, ag_ref, ab_ref,
                       bg_ref, bb_ref, sw_ref, sb_ref, y_ref, ain_ref, vout_ref, *, kw, groups, width):
    n = val_ref.shape[0]
    y_ref[...] = jnp.zeros(y_ref.shape, y_ref.dtype)
    a_in = val_ref[...] * jax.nn.sigmoid(gate_ref[...])
    ain_ref[...] = a_in
    acc = cw_ref[0:1, :] * st_ref[0]
    for k in range(1, kw - 1):
        acc = acc + cw_ref[k:k + 1, :] * st_ref[k]
    acc = acc + cw_ref[kw - 1:kw, :] * a_in + cb_ref[...]
    for g in range(groups):
        cols = slice(g * LANES, (g + 1) * LANES)
        y = _layer_norm(acc[:, cols], ag_ref[:, cols], ab_ref[:, cols])
        y_ref[0:n, cols] = _silu(y)
    v = _layer_norm(_gelu(v_ref[...]), bg_ref[...], bb_ref[...])
    vout_ref[...] = v
    y_ref[0:n, width:2 * width] = _gelu(u_ref[...]) * (sw_ref[...] * v + sb_ref[...])


def _mix_sample(z, state_t, conv_w, conv_b, a_g, a_b, b_g, b_b, s_w0, s_b0, layer, *, row0, n, width):
    tail = z.shape[0] - row0
    kw = conv_w.shape[1]
    rb = row0 // n
    zspec = lambda c: pl.BlockSpec((n, width), lambda i: (rb, c))
    vec = lambda: pl.BlockSpec((None, 1, width), lambda i: (layer, 0, 0))
    return pl.pallas_call(
        functools.partial(_mix_sample_kernel, kw=kw, groups=A_GROUPS, width=width),
        grid=(1,),
        in_specs=[
            zspec(0), zspec(1), zspec(2), zspec(3),
            pl.BlockSpec((kw - 1, n, width), lambda i: (0, 0, 0)),
            pl.BlockSpec((None, kw, width), lambda i: (layer, 0, 0)),
            vec(), vec(), vec(), vec(), vec(), vec(), vec(),
        ],
        out_specs=[
            pl.BlockSpec((tail, 2 * width), lambda i: (0, 0)),
            pl.BlockSpec((n, width), lambda i: (0, 0)),
            pl.BlockSpec((n, width), lambda i: (0, 0)),
        ],
        out_shape=[
            jax.ShapeDtypeStruct((tail, 2 * width), F32),
            jax.ShapeDtypeStruct((n, width), F32),
            jax.ShapeDtypeStruct((n, width), F32),
        ],
        compiler_params=_params(("arbitrary",)),
    )(z, z, z, z, state_t, conv_w, conv_b, a_g, a_b, b_g, b_b, s_w0, s_b0)


def _rows(start, size, stride):
    return pl.ds(start, size) if stride == 1 else pl.ds(start, size, stride=stride)


LOG2E = 1.4426950408889634


def _attn_prompt_kernel(q_ref, k_ref, v_ref, tail_ref, o_ref, bias1_ref, bias2_ref, *stats,
                        batch, patterns, blk, scale):
    @pl.when(pl.program_id(0) == batch)
    def _():
        o_ref[0:tail_ref.shape[0], :] = tail_ref[...].astype(o_ref.dtype)

    @pl.when(pl.program_id(0) < batch)
    def _():
        _attn_prompt_sequence(q_ref, k_ref, v_ref, o_ref, bias1_ref, bias2_ref, stats, patterns, blk, scale)


def _attn_prompt_sequence(q_ref, k_ref, v_ref, o_ref, bias1_ref, bias2_ref, stats, patterns, blk, scale):
    seq, e = q_ref.shape
    npat = len(patterns)
    acc_refs, m_refs, l_refs = stats[:npat], stats[npat:2 * npat], stats[2 * npat:]
    c = scale * LOG2E

    def distance(nk):
        qi = lax.broadcasted_iota(jnp.int32, (blk, nk), 0)
        kj = lax.broadcasted_iota(jnp.int32, (blk, nk), 1)
        return (nk - blk) + qi - kj

    def attend(q_rows, k_rows, bias):
        qb = q_ref[q_rows, :].astype(BF16)
        kb = k_ref[k_rows, :].astype(BF16)
        vb = v_ref[k_rows, :].astype(BF16)
        s = lax.dot_general(qb, kb, (((1,), (1,)), ((), ())), preferred_element_type=F32) + bias
        m = jnp.max(s, axis=-1, keepdims=True)
        p = jnp.exp2((s - m) * c).astype(BF16)
        ov = jnp.dot(p, jnp.concatenate([vb, jnp.ones_like(vb)], axis=1), preferred_element_type=F32)
        return ov[:, :e], ov[:, e:], jnp.broadcast_to(m, (blk, e))

    for p, (window, dil) in enumerate(patterns):
        reach = window // dil
        nblk = seq // dil // blk
        for bias_ref in (bias1_ref, bias2_ref):
            dist = distance(bias_ref.shape[1])
            bias_ref[...] = jnp.where((dist >= 0) & (dist <= reach), 0.0, -jnp.inf)
        for r in range(dil):
            for b in range(nblk):
                q_rows = _rows(r + dil * blk * b, blk, dil)
                if b == 0:
                    k_rows, bias = q_rows, bias1_ref[...]
                else:
                    k_rows, bias = _rows(r + dil * blk * (b - 1), 2 * blk, dil), bias2_ref[...]
                acc_refs[p][q_rows, :], l_refs[p][q_rows, :], m_refs[p][q_rows, :] = attend(q_rows, k_rows, bias)

    def merge(t, carry):
        rows = pl.ds(pl.multiple_of(t * blk, blk), blk)
        ms = [m_ref[rows, :] for m_ref in m_refs]
        top = functools.reduce(jnp.maximum, ms)
        ws = [jnp.exp2((m - top) * c) for m in ms]
        den = functools.reduce(lambda x, y: x + y, [w * l_ref[rows, :] for w, l_ref in zip(ws, l_refs)])
        num = functools.reduce(lambda x, y: x + y, [w * a_ref[rows, :] for w, a_ref in zip(ws, acc_refs)])
        o_ref[rows, :] = (num / den).astype(o_ref.dtype)
        return carry

    lax.fori_loop(0, seq // blk, merge, 0)


def _attn_prompt(qkv, tail, *, batch, seq, heads):
    e = qkv.shape[1] // (3 * heads)
    assert e == LANES and tail.shape[0] == qkv.shape[0] - batch * seq and tail.shape[0] <= seq
    for window, dil in C_PATTERNS:
        assert window // dil <= C_BLOCK and (seq // dil) % C_BLOCK == 0
    spec = pl.BlockSpec((seq, e), lambda b, h: (b, h))
    part = lambda c: pl.BlockSpec((seq, e), lambda b, h: (b, c * heads + h))
    nstats = 3 * len(C_PATTERNS)
    return pl.pallas_call(
        functools.partial(_attn_prompt_kernel, batch=batch, patterns=C_PATTERNS, blk=C_BLOCK, scale=e ** -0.5),
        grid=(batch + 1, heads),
        in_specs=[part(0), part(1), part(2), pl.BlockSpec((tail.shape[0], e), lambda b, h: (0, h))],
        out_specs=spec,
        out_shape=jax.ShapeDtypeStruct((qkv.shape[0], heads * e), BF16),
        scratch_shapes=(
            [pltpu.VMEM((C_BLOCK, C_BLOCK), F32), pltpu.VMEM((C_BLOCK, 2 * C_BLOCK), F32)]
            + [pltpu.VMEM((seq, e), F32) for _ in range(nstats)]
        ),
        compiler_params=_params(("parallel", "parallel")),
    )(qkv, qkv, qkv, tail)


def _attn_sample_kernel(q_ref, kn_ref, vn_ref, *refs, npat, scale):
    kc_refs, vc_refs, o_ref = refs[:npat], refs[npat:2 * npat], refs[2 * npat]
    q = q_ref[...]
    s_new = jnp.sum(q * kn_ref[...], axis=-1, keepdims=True) * scale
    scores = [jnp.sum(kc[...] * q[None], axis=-1, keepdims=True) * scale for kc in kc_refs]
    top = s_new
    for s in scores:
        top = jnp.maximum(top, jnp.max(s, axis=0))
    w_new = npat * jnp.exp(s_new - top)
    den = w_new
    num = w_new * vn_ref[...]
    for s, vc in zip(scores, vc_refs):
        p = jnp.exp(s - top[None])
        den = den + jnp.sum(p, axis=0)
        num = num + jnp.sum(p * vc[...], axis=0)
    o_ref[...] = num / den


def _attn_sample(q_s, k_s, v_s, cache_k, cache_v, layer, *, head_group=8):
    n, heads, e = q_s.shape
    hist = cache_k.shape[2]
    new = pl.BlockSpec((None, head_group, e), lambda i, g: (i, g, 0))
    views, specs = [], []
    for window, dil in C_PATTERNS:
        reach = window // dil
        assert window <= hist and hist % dil == 0 and (hist // dil) % reach == 0
        last = hist // dil // reach - 1
        views.append((cache_k.shape[0], n, hist // dil, dil, heads, e))
        specs.append(pl.BlockSpec((None, None, reach, None, head_group, e),
                                  lambda i, g, last=last: (layer, i, last, 0, g, 0)))
    return pl.pallas_call(
        functools.partial(_attn_sample_kernel, npat=len(C_PATTERNS), scale=e ** -0.5),
        grid=(n, heads // head_group),
        in_specs=[new, new, new] + specs + specs,
        out_specs=new,
        out_shape=jax.ShapeDtypeStruct((n, heads, e), F32),
        compiler_params=_params(("parallel", "parallel")),
    )(q_s, k_s, v_s, *[cache_k.reshape(v) for v in views], *[cache_v.reshape(v) for v in views])


def _cache_set_last_kernel(new_ref, shifted_ref, o_ref):
    del shifted_ref
    o_ref[...] = new_ref[...]


def _cache_set_last(shifted, new):
    n, hist, heads, e = shifted.shape
    return pl.pallas_call(
        _cache_set_last_kernel,
        grid=(n,),
        in_specs=[pl.BlockSpec((None, 1, heads, e), lambda i: (i, 0, 0, 0)), pl.BlockSpec(memory_space=pl.ANY)],
        out_specs=pl.BlockSpec((None, 1, heads, e), lambda i: (i, hist - 1, 0, 0)),
        out_shape=jax.ShapeDtypeStruct(shifted.shape, shifted.dtype),
        input_output_aliases={1: 0},
        compiler_params=_params(("parallel",)),
    )(new, shifted)


def kernel(x_prompt, x_sample, state_a_conv, cache_c_k, cache_c_v, ln_mix_even, w_in_even, a_conv_w,
           a_conv_b, a_norm_g, a_norm_b, b_norm_g, b_norm_b, b_spatial_w, b_spatial_b, w_out_even,
           ln_mix_odd, w_qkv_odd, w_o_odd, ln_ffn, w_gate_up, w_down, ln_final):
    batch, seq, d = x_prompt.shape
    nsamp = x_sample.shape[0] * x_sample.shape[1]
    depth = ln_ffn.shape[0]
    np_rows = batch * seq
    rows_all = pl.cdiv(np_rows + nsamp, ROW_TILE) * ROW_TILE
    tail = rows_all - np_rows
    assert rows_all % RMS_TILE == 0 and np_rows % RMS_TILE_OUT == 0 and np_rows % nsamp == 0
    width = a_conv_w.shape[-1]
    ffn = w_down.shape[1]
    e = d // C_HEADS

    def tail_rows(s):
        return jnp.concatenate([s, jnp.zeros((tail - nsamp, s.shape[1]), s.dtype)], axis=0)

    vec3 = lambda t: t.reshape(t.shape[0], 1, t.shape[-1])
    x = jnp.concatenate([x_prompt.reshape(np_rows, d), tail_rows(x_sample.reshape(nsamp, d))], axis=0)
    rms = functools.partial(_rmsnorm, out_dtype=BF16, rows=rows_all, row_tile=RMS_TILE)

    a_hist_p, a_hist_s, b_v_s = [], [], []
    c_k_p, c_v_p = [], []
    rides = [(c, j) for j in range(cache_c_k.shape[0]) for c in (cache_c_k, cache_c_v)]
    assert len(rides) <= depth
    shifted, new_rows = [], []
    for i in range(depth):
        j = i // 2
        if i % 2 == 0:
            h = rms(x, vec3(ln_mix_even), j)
            z = _matmul([h], w_in_even, j, n=w_in_even.shape[-1])
            s_w0 = jnp.repeat(b_spatial_w[:, :, 0, 0], width // B_GROUPS, axis=-1)
            s_b0 = jnp.repeat(b_spatial_b[:, :, 0], width // B_GROUPS, axis=-1)
            y_s, a_in_s, v_s = _mix_sample(
                z, jnp.swapaxes(state_a_conv[j], 0, 1), a_conv_w, vec3(a_conv_b), vec3(a_norm_g),
                vec3(a_norm_b), vec3(b_norm_g), vec3(b_norm_b), vec3(s_w0), vec3(s_b0), j,
                row0=np_rows, n=nsamp, width=width)
            a, hist_p = _mixa_prompt(z, y_s, a_conv_w, vec3(a_conv_b), vec3(a_norm_g), vec3(a_norm_b), j,
                                     batch=batch, seq=seq, width=width)
            b = _mixb_prompt(z, y_s, vec3(b_norm_g), vec3(b_norm_b), b_spatial_w,
                             jnp.swapaxes(b_spatial_b, 1, 2), j, rows=np_rows, width=width)
            x = _matmul([a, b], w_out_even, j, n=d, res=x)
            a_hist_p.append(hist_p)
            a_hist_s.append(jnp.concatenate([state_a_conv[j][:, 1:], a_in_s[:, None, :]], axis=1))
            b_v_s.append(v_s.reshape(x_sample.shape[0], x_sample.shape[1], width))
        else:
            h = rms(x, vec3(ln_mix_odd), j)
            qkv = _matmul([h], w_qkv_odd, j, n=3 * d)
            qkv_s = qkv[np_rows:np_rows + nsamp].reshape(nsamp, 3, C_HEADS, e)
            o_s = _attn_sample(qkv_s[:, 0], qkv_s[:, 1], qkv_s[:, 2], cache_c_k, cache_c_v, j)
            o = _attn_prompt(qkv, tail_rows(o_s.reshape(nsamp, d)), batch=batch, seq=seq, heads=C_HEADS)
            x = _matmul([o], w_o_odd, j, n=d, res=x)
            keep = min(C_PATTERNS[-1][0], seq)
            c_k_p.append(qkv[:np_rows, d:2 * d].reshape(batch, seq, C_HEADS, e)[:, seq - keep:])
            c_v_p.append(qkv[:np_rows, 2 * d:].reshape(batch, seq, C_HEADS, e)[:, seq - keep:])
            new_rows += [qkv_s[:, 1:2], qkv_s[:, 2:3]]
        h = rms(x, vec3(ln_ffn), i)
        hid, bulk = _matmul_gated(h, w_gate_up, i, half=ffn, ride=rides[i] if i < len(rides) else None)
        shifted.append(bulk)
        x = _matmul([hid], w_down, i, n=d, tm=DOWN_ROW_TILE, tn=GATE_TILE, nk=2, res=x)

    c_s = [_cache_set_last(bulk, new) for bulk, new in zip(shifted, new_rows)]
    c_k_s, c_v_s = c_s[0::2], c_s[1::2]
    g_final = ln_final.reshape(1, 1, d)
    y_prompt = _rmsnorm(x, g_final, 0, F32, rows=np_rows, row_tile=RMS_TILE_OUT)
    y_sample = _rmsnorm(x, g_final, 0, F32, rows=nsamp, row_tile=nsamp, first_block=np_rows // nsamp)
    return (y_prompt.reshape(batch, seq, d), y_sample.reshape(x_sample.shape),
            jnp.stack(a_hist_p), jnp.stack(a_hist_s), jnp.stack(b_v_s),
            jnp.stack(c_k_p), jnp.stack(c_v_p), jnp.stack(c_k_s), jnp.stack(c_v_s))
```

```python
import functools

import jax
import jax.numpy as jnp
from jax import lax
from jax.experimental import pallas as pl
from jax.experimental.pallas import tpu as pltpu

F32 = jnp.float32
BF16 = jnp.bfloat16

LANES = 128
VMEM_BYTES = 64 * 1024 * 1024
VMEM_LIMIT = VMEM_BYTES - 8 * 1024 * 1024

RMS_EPS = 1e-6
LN_EPS = 1e-5

A_GROUPS = 16
B_GROUPS = 16
B_CHUNK = 128
C_HEADS = 32
C_PATTERNS = ((128, 1), (512, 4), (2048, 16))
C_BLOCK = 128

ROW_TILE = 2080
DOWN_ROW_TILE = 1040
N_TILE = 512
GATE_TILE = 256
NORM_ROW_TILE = 1664
RMS_TILE_OUT = 256


def _params(semantics):
    return pltpu.CompilerParams(dimension_semantics=semantics, vmem_limit_bytes=VMEM_LIMIT)


def _rms_kernel(x_ref, g_ref, o_ref):
    x = x_ref[...]
    y = x * lax.rsqrt(jnp.mean(x * x, axis=-1, keepdims=True) + RMS_EPS)
    o_ref[...] = (y * g_ref[...]).astype(o_ref.dtype)


def _rmsnorm(x, gains, layer, out_dtype, *, rows, row_tile, first_block=0):
    d = x.shape[1]
    return pl.pallas_call(
        _rms_kernel,
        grid=(rows // row_tile,),
        in_specs=[
            pl.BlockSpec((row_tile, d), lambda i: (i + first_block, 0)),
            pl.BlockSpec((None, 1, d), lambda i: (layer, 0, 0)),
        ],
        out_specs=pl.BlockSpec((row_tile, d), lambda i: (i, 0)),
        out_shape=jax.ShapeDtypeStruct((rows, d), out_dtype),
        compiler_params=_params(("parallel",)),
    )(x, gains)


def _stack_kernel(x_ref, tail_ref, gain_ref, o_ref, ob_ref, rstd_ref, *, nblk):
    def emit(x):
        rows = x.shape[0]
        o_ref[0:rows, :] = x
        ob_ref[0:rows, :] = (x * gain_ref[...]).astype(ob_ref.dtype)
        rstd = lax.rsqrt(jnp.mean(x * x, axis=-1, keepdims=True) + RMS_EPS)
        rstd_ref[0:rows, :] = jnp.broadcast_to(rstd, (rows, LANES))

    @pl.when(pl.program_id(0) < nblk)
    def _():
        emit(x_ref[...])

    @pl.when(pl.program_id(0) == nblk)
    def _():
        emit(tail_ref[...])


def _stack_rows(x_prompt, tail, gains, gain_idx, *, row_tile=RMS_TILE_OUT):
    p, d = x_prompt.shape
    t = tail.shape[0]
    assert p % row_tile == 0 and t <= row_tile
    nblk = p // row_tile
    rows = lambda width: pl.BlockSpec((row_tile, width), lambda i: (i, 0))
    return pl.pallas_call(
        functools.partial(_stack_kernel, nblk=nblk),
        grid=(nblk + 1,),
        in_specs=[pl.BlockSpec((row_tile, d), lambda i: (jnp.minimum(i, nblk - 1), 0)),
                  pl.BlockSpec((t, d), lambda i: (0, 0)),
                  pl.BlockSpec((None, 1, d), lambda i: (gain_idx, 0, 0))],
        out_specs=[rows(d), rows(d), rows(LANES)],
        out_shape=[jax.ShapeDtypeStruct((p + t, d), F32), jax.ShapeDtypeStruct((p + t, d), BF16),
                   jax.ShapeDtypeStruct((p + t, LANES), F32)],
        compiler_params=_params(("parallel",)),
    )(x_prompt, tail, gains)


def _mm_kernel(*refs, a_widths, nk, tk, has_rstd, has_res, emit_norm):
    it = iter(refs)
    a_refs = [next(it) for _ in a_widths]
    b_ref = next(it)
    rstd_ref = next(it) if has_rstd else None
    res_ref = next(it) if has_res else None
    gain_ref = next(it) if emit_norm else None
    o_ref = next(it)
    ob_ref, rstd_out_ref = (next(it), next(it)) if emit_norm else (None, None)
    acc_ref = next(it) if nk > 1 else None
    ss_ref = next(it) if emit_norm else None
    j, nj = pl.program_id(1), pl.num_programs(1)

    def finish(acc):
        if has_rstd:
            acc = acc * rstd_ref[:, 0:1]
        if has_res:
            acc = acc + res_ref[...]
        o_ref[...] = acc.astype(o_ref.dtype)
        if emit_norm:
            ob_ref[...] = (acc * gain_ref[...]).astype(ob_ref.dtype)
            sq = acc * acc
            part = sq[:, 0:LANES]
            for c in range(1, acc.shape[1] // LANES):
                part = part + sq[:, c * LANES:(c + 1) * LANES]

            @pl.when(j == 0)
            def _():
                ss_ref[...] = part

            @pl.when(j > 0)
            def _():
                ss_ref[...] += part

            @pl.when(j == nj - 1)
            def _():
                mean_sq = jnp.sum(ss_ref[...], axis=-1, keepdims=True) / (nj * acc.shape[1])
                rstd_out_ref[...] = jnp.broadcast_to(lax.rsqrt(mean_sq + RMS_EPS), rstd_out_ref.shape)

    if nk == 1:
        acc, k0 = None, 0
        for a_ref, kw in zip(a_refs, a_widths):
            part = jnp.dot(a_ref[...], b_ref[k0:k0 + kw, :].astype(BF16), preferred_element_type=F32)
            acc = part if acc is None else acc + part
            k0 += kw
        finish(acc)
        return

    (a_ref,) = a_refs
    k = pl.program_id(2)
    for kk in range(nk):
        @pl.when(k == kk)
        def _(kk=kk):
            part = jnp.dot(a_ref[:, kk * tk:(kk + 1) * tk], b_ref[...].astype(BF16),
                           preferred_element_type=F32)
            if kk == 0:
                acc_ref[...] = part
            elif kk < nk - 1:
                acc_ref[...] += part
            else:
                finish(acc_ref[...] + part)


def _matmul(a_parts, w, layer, *, n, tm, tn=N_TILE, nk=1, rstd=None, res=None, next_gain=None):
    emit_norm = next_gain is not None
    m = a_parts[0].shape[0]
    a_widths = tuple(a.shape[1] for a in a_parts)
    kdim = sum(a_widths)
    assert nk == 1 or len(a_parts) == 1
    assert w.shape[-1] == n or not emit_norm
    tk = kdim // nk
    in_specs = [pl.BlockSpec((tm, kw), lambda i, j, k: (i, 0), pipeline_mode=pl.Buffered(1)) for kw in a_widths]
    in_specs.append(pl.BlockSpec((None, tk, tn), lambda i, j, k: (layer, k, j)))
    args = [*a_parts, w]
    if rstd is not None:
        in_specs.append(pl.BlockSpec((tm, LANES), lambda i, j, k: (i, 0)))
        args.append(rstd)
    if res is not None:
        in_specs.append(pl.BlockSpec((tm, tn), lambda i, j, k: (i, j)))
        args.append(res)
    if emit_norm:
        gains, gain_idx = next_gain
        in_specs.append(pl.BlockSpec((None, 1, tn), lambda i, j, k: (gain_idx, 0, j)))
        args.append(gains)
    tile = pl.BlockSpec((tm, tn), lambda i, j, k: (i, j))
    out_specs, out_shape = [tile], [jax.ShapeDtypeStruct((m, n), F32)]
    scratch = [pltpu.VMEM((tm, tn), F32)] if nk > 1 else []
    if emit_norm:
        out_specs += [tile, pl.BlockSpec((tm, LANES), lambda i, j, k: (i, 0))]
        out_shape += [jax.ShapeDtypeStruct((m, n), BF16), jax.ShapeDtypeStruct((m, LANES), F32)]
        scratch.append(pltpu.VMEM((tm, LANES), F32))
    outs = pl.pallas_call(
        functools.partial(_mm_kernel, a_widths=a_widths, nk=nk, tk=tk, has_rstd=rstd is not None,
                          has_res=res is not None, emit_norm=emit_norm),
        grid=(m // tm, n // tn, nk),
        in_specs=in_specs,
        out_specs=out_specs,
        out_shape=out_shape,
        scratch_shapes=scratch,
        compiler_params=_params(("parallel", "arbitrary", "arbitrary")),
    )(*args)
    return outs if emit_norm else outs[0]


def _mm_gated_kernel(a_ref, bg_ref, bu_ref, rstd_ref, *refs, nj, ride_steps):
    o_ref = refs[-1] if ride_steps == 0 else refs[-2]
    a = a_ref[...]
    rstd = rstd_ref[:, 0:1]
    g = jnp.dot(a, bg_ref[...].astype(BF16), preferred_element_type=F32) * rstd
    u = jnp.dot(a, bu_ref[...].astype(BF16), preferred_element_type=F32) * rstd
    o_ref[...] = (g * jax.nn.sigmoid(g) * u).astype(o_ref.dtype)

    if ride_steps:
        old_ref, next_ref, shifted_ref = refs[0], refs[1], refs[-1]
        rows = shifted_ref.shape[0]

        @pl.when(pl.program_id(0) * nj + pl.program_id(1) < ride_steps)
        def _():
            shifted_ref[0:rows - 1] = old_ref[1:rows]
            shifted_ref[rows - 1] = next_ref[0]


def _matmul_gated(a, w, layer, rstd, *, half, tm=ROW_TILE, tn=GATE_TILE, ride=None, ride_rows=128):
    m, kdim = a.shape
    ni, nj = m // tm, half // tn
    in_specs = [
        pl.BlockSpec((tm, kdim), lambda i, j: (i, 0), pipeline_mode=pl.Buffered(1)),
        pl.BlockSpec((None, kdim, tn), lambda i, j: (layer, 0, j)),
        pl.BlockSpec((None, kdim, tn), lambda i, j: (layer, 0, j + nj)),
        pl.BlockSpec((tm, LANES), lambda i, j: (i, 0)),
    ]
    args = [a, w, w, rstd]
    out_specs = [pl.BlockSpec((tm, tn), lambda i, j: (i, j))]
    out_shape = [jax.ShapeDtypeStruct((m, half), BF16)]
    ride_steps = 0
    if ride is not None:
        cache, cache_layer = ride
        _, n, hist, heads, e = cache.shape
        assert hist % ride_rows == 0
        nblk = hist // ride_rows
        ride_steps = n * nblk
        assert ride_steps <= ni * nj

        def seq_blk(i, j):
            step = jnp.minimum(i * nj + j, ride_steps - 1)
            return step // nblk, step % nblk

        def old_map(i, j):
            sq, t = seq_blk(i, j)
            return cache_layer, sq, t, 0, 0

        def next_map(i, j):
            sq, t = seq_blk(i, j)
            return cache_layer, sq, jnp.minimum((t + 1) * ride_rows, hist - 1), 0, 0

        def out_map(i, j):
            sq, t = seq_blk(i, j)
            return sq, t, 0, 0

        in_specs += [pl.BlockSpec((None, None, ride_rows, heads, e), old_map),
                     pl.BlockSpec((None, None, 1, heads, e), next_map)]
        args += [cache, cache]
        out_specs.append(pl.BlockSpec((None, ride_rows, heads, e), out_map))
        out_shape.append(jax.ShapeDtypeStruct(cache.shape[1:], cache.dtype))
    outs = pl.pallas_call(
        functools.partial(_mm_gated_kernel, nj=nj, ride_steps=ride_steps),
        grid=(ni, nj),
        in_specs=in_specs,
        out_specs=out_specs,
        out_shape=out_shape,
        compiler_params=_params(("arbitrary", "arbitrary") if ride_steps else ("parallel", "parallel")),
    )(*args)
    return outs if ride_steps else (outs[0], None)


def _gelu(x):
    return 0.5 * x * (1.0 + lax.erf(x * (2.0 ** -0.5)))


def _layer_norm(x, g, b):
    xc = x - jnp.mean(x, axis=-1, keepdims=True)
    y = xc * lax.rsqrt(jnp.mean(xc * xc, axis=-1, keepdims=True) + LN_EPS)
    return y * g + b


def _silu(x):
    return x * jax.nn.sigmoid(x)


def _mixa_prompt_kernel(val_ref, gate_ref, w_ref, cb_ref, g_ref, b_ref, tail_ref, a_ref, st_ref, ext_ref,
                        *, batch, seq, kw, hist_pad, chunk):
    b = pl.program_id(1)

    @pl.when(b == batch)
    def _():
        a_ref[0:tail_ref.shape[0], :] = tail_ref[...].astype(a_ref.dtype)

    @pl.when(b < batch)
    def _():
        ext_ref[0:hist_pad, :] = jnp.zeros((hist_pad, LANES), F32)
        ext_ref[hist_pad:hist_pad + seq, :] = val_ref[...] * jax.nn.sigmoid(gate_ref[...])
        st_ref[...] = ext_ref[hist_pad + seq - (kw - 1):hist_pad + seq, :]
        off = hist_pad - (kw - 1)
        for c in range(seq // chunk):
            t0 = c * chunk
            acc = w_ref[0:1, :] * ext_ref[t0 + off:t0 + off + chunk, :]
            for k in range(1, kw):
                acc = acc + w_ref[k:k + 1, :] * ext_ref[t0 + off + k:t0 + off + k + chunk, :]
            y = _layer_norm(acc + cb_ref[...], g_ref[...], b_ref[...])
            a_ref[t0:t0 + chunk, :] = _silu(y).astype(a_ref.dtype)


def _mixa_prompt(z, tail, conv_w, conv_b, norm_g, norm_b, layer, *, batch, seq, width):
    kw = conv_w.shape[1]
    nc = width // LANES
    hist_pad = 32
    assert tail.shape[0] == z.shape[0] - batch * seq and tail.shape[0] <= seq
    vec = lambda: pl.BlockSpec((None, 1, LANES), lambda c, b: (layer, 0, c))
    return pl.pallas_call(
        functools.partial(_mixa_prompt_kernel, batch=batch, seq=seq, kw=kw, hist_pad=hist_pad, chunk=128),
        grid=(nc, batch + 1),
        in_specs=[
            pl.BlockSpec((seq, LANES), lambda c, b: (b, c)),
            pl.BlockSpec((seq, LANES), lambda c, b: (b, c + nc)),
            pl.BlockSpec((None, kw, LANES), lambda c, b: (layer, 0, c)),
            vec(), vec(), vec(),
            pl.BlockSpec((tail.shape[0], LANES), lambda c, b: (0, c)),
        ],
        out_specs=[
            pl.BlockSpec((seq, LANES), lambda c, b: (b, c)),
            pl.BlockSpec((None, kw - 1, LANES), lambda c, b: (jnp.minimum(b, batch - 1), 0, c)),
        ],
        out_shape=[
            jax.ShapeDtypeStruct((z.shape[0], width), BF16),
            jax.ShapeDtypeStruct((batch, kw - 1, width), F32),
        ],
        scratch_shapes=[pltpu.VMEM((hist_pad + seq, LANES), F32)],
        compiler_params=_params(("parallel", "arbitrary")),
    )(z, z, conv_w, conv_b, norm_g, norm_b, tail)


def _mixb_prompt_kernel(u_ref, v_ref, ng_ref, nb_ref, ws_ref, bst_ref, tail_ref, o_ref, *, groups, nchunks):
    step = pl.program_id(0)

    @pl.when(step == nchunks)
    def _():
        o_ref[...] = tail_ref[...].astype(o_ref.dtype)

    @pl.when(step < nchunks)
    def _():
        v = _layer_norm(_gelu(v_ref[...]), ng_ref[...], nb_ref[...]).astype(BF16)
        n = ws_ref.shape[-1]
        causal = lax.broadcasted_iota(jnp.int32, (n, n), 1) <= lax.broadcasted_iota(jnp.int32, (n, n), 0)
        for g in range(groups):
            cols = slice(g * LANES, (g + 1) * LANES)
            w_m = jnp.where(causal, ws_ref[g], 0.0).astype(BF16)
            s = jnp.dot(w_m, v[:, cols], preferred_element_type=F32) + bst_ref[:, g:g + 1]
            o_ref[:, cols] = (_gelu(u_ref[:, cols]) * s).astype(o_ref.dtype)


def _mixb_prompt(z, tail, norm_g, norm_b, spatial_w, spatial_bt, layer, *, rows, width):
    groups, chunk = spatial_w.shape[1], spatial_w.shape[2]
    assert tail.shape[0] == z.shape[0] - rows == chunk
    vec = lambda: pl.BlockSpec((None, 1, width), lambda s: (layer, 0, 0))
    return pl.pallas_call(
        functools.partial(_mixb_prompt_kernel, groups=groups, nchunks=rows // chunk),
        grid=(rows // chunk + 1,),
        in_specs=[
            pl.BlockSpec((chunk, width), lambda s: (s, 2)),
            pl.BlockSpec((chunk, width), lambda s: (s, 3)),
            vec(), vec(),
            pl.BlockSpec((None, groups, chunk, chunk), lambda s: (layer, 0, 0, 0)),
            pl.BlockSpec((None, chunk, groups), lambda s: (layer, 0, 0)),
            pl.BlockSpec((chunk, width), lambda s: (0, 1)),
        ],
        out_specs=pl.BlockSpec((chunk, width), lambda s: (s, 0)),
        out_shape=jax.ShapeDtypeStruct((z.shape[0], width), BF16),
        compiler_params=_params(("parallel",)),
    )(z, z, norm_g, norm_b, spatial_w, spatial_bt, tail)


def _mix_sample_kernel(val_ref, gate_ref, u_ref, v_ref, st_ref, cw_ref, cb_ref, ag_ref, ab_ref,
                       bg_ref, bb_ref, sw_ref, sb_ref, y_ref, ain_ref, vout_ref, *, kw, groups, width):
    n = val_ref.shape[0]
    y_ref[...] = jnp.zeros(y_ref.shape, y_ref.dtype)
    a_in = val_ref[...] * jax.nn.sigmoid(gate_ref[...])
    ain_ref[...] = a_in
    acc = cw_ref[0:1, :] * st_ref[0]
    for k in range(1, kw - 1):
        acc = acc + cw_ref[k:k + 1, :] * st_ref[k]
    acc = acc + cw_ref[kw - 1:kw, :] * a_in + cb_ref[...]
    for g in range(groups):
        cols = slice(g * LANES, (g + 1) * LANES)
        y = _layer_norm(acc[:, cols], ag_ref[:, cols], ab_ref[:, cols])
        y_ref[0:n, cols] = _silu(y)
    v = _layer_norm(_gelu(v_ref[...]), bg_ref[...], bb_ref[...])
    vout_ref[...] = v
    y_ref[0:n, width:2 * width] = _gelu(u_ref[...]) * (sw_ref[...] * v + sb_ref[...])


def _mix_sample(z, state_t, conv_w, conv_b, a_g, a_b, b_g, b_b, s_w0, s_b0, layer, *, row0, n, width):
    tail = z.shape[0] - row0
    kw = conv_w.shape[1]
    rb = row0 // n
    zspec = lambda c: pl.BlockSpec((n, width), lambda i: (rb, c))
    vec = lambda: pl.BlockSpec((None, 1, width), lambda i: (layer, 0, 0))
    return pl.pallas_call(
        functools.partial(_mix_sample_kernel, kw=kw, groups=A_GROUPS, width=width),
        grid=(1,),
        in_specs=[
            zspec(0), zspec(1), zspec(2), zspec(3),
            pl.BlockSpec((kw - 1, n, width), lambda i: (0, 0, 0)),
            pl.BlockSpec((None, kw, width), lambda i: (layer, 0, 0)),
            vec(), vec(), vec(), vec(), vec(), vec(), vec(),
        ],
        out_specs=[
            pl.BlockSpec((tail, 2 * width), lambda i: (0, 0)),
            pl.BlockSpec((n, width), lambda i: (0, 0)),
            pl.BlockSpec((n, width), lambda i: (0, 0)),
        ],
        out_shape=[
            jax.ShapeDtypeStruct((tail, 2 * width), F32),
            jax.ShapeDtypeStruct((n, width), F32),
            jax.ShapeDtypeStruct((n, width), F32),
        ],
        compiler_params=_params(("arbitrary",)),
    )(z, z, z, z, state_t, conv_w, conv_b, a_g, a_b, b_g, b_b, s_w0, s_b0)


def _rows(start, size, stride):
    return pl.ds(start, size) if stride == 1 else pl.ds(start, size, stride=stride)


LOG2E = 1.4426950408889634


def _attn_prompt_kernel(q_ref, k_ref, v_ref, tail_ref, o_ref, bias1_ref, bias2_ref, *stats,
                        batch, patterns, blk, scale):
    @pl.when(pl.program_id(0) == batch)
    def _():
        o_ref[0:tail_ref.shape[0], :] = tail_ref[...].astype(o_ref.dtype)

    @pl.when(pl.program_id(0) < batch)
    def _():
        _attn_prompt_sequence(q_ref, k_ref, v_ref, o_ref, bias1_ref, bias2_ref, stats, patterns, blk, scale)


def _attn_prompt_sequence(q_ref, k_ref, v_ref, o_ref, bias1_ref, bias2_ref, stats, patterns, blk, scale):
    seq, e = q_ref.shape
    npat = len(patterns)
    acc_refs, m_refs, l_refs = stats[:npat], stats[npat:2 * npat], stats[2 * npat:]
    c = scale * LOG2E

    def distance(nk):
        qi = lax.broadcasted_iota(jnp.int32, (blk, nk), 0)
        kj = lax.broadcasted_iota(jnp.int32, (blk, nk), 1)
        return (nk - blk) + qi - kj

    def attend(q_rows, k_rows, bias):
        qb = q_ref[q_rows, :].astype(BF16)
        kb = k_ref[k_rows, :].astype(BF16)
        vb = v_ref[k_rows, :].astype(BF16)
        s = lax.dot_general(qb, kb, (((1,), (1,)), ((), ())), preferred_element_type=F32) + bias
        m = jnp.max(s, axis=-1, keepdims=True)
        p = jnp.exp2((s - m) * c).astype(BF16)
        ov = jnp.dot(p, jnp.concatenate([vb, jnp.ones_like(vb)], axis=1), preferred_element_type=F32)
        return ov[:, :e], ov[:, e:], jnp.broadcast_to(m, (blk, e))

    for p, (window, dil) in enumerate(patterns):
        reach = window // dil
        nblk = seq // dil // blk
        for bias_ref in (bias1_ref, bias2_ref):
            dist = distance(bias_ref.shape[1])
            bias_ref[...] = jnp.where((dist >= 0) & (dist <= reach), 0.0, -jnp.inf)
        for r in range(dil):
            for b in range(nblk):
                q_rows = _rows(r + dil * blk * b, blk, dil)
                if b == 0:
                    k_rows, bias = q_rows, bias1_ref[...]
                else:
                    k_rows, bias = _rows(r + dil * blk * (b - 1), 2 * blk, dil), bias2_ref[...]
                acc_refs[p][q_rows, :], l_refs[p][q_rows, :], m_refs[p][q_rows, :] = attend(q_rows, k_rows, bias)

    def merge(t, carry):
        rows = pl.ds(pl.multiple_of(t * blk, blk), blk)
        ms = [m_ref[rows, :] for m_ref in m_refs]
        top = functools.reduce(jnp.maximum, ms)
        ws = [jnp.exp2((m - top) * c) for m in ms]
        den = functools.reduce(lambda x, y: x + y, [w * l_ref[rows, :] for w, l_ref in zip(ws, l_refs)])
        num = functools.reduce(lambda x, y: x + y, [w * a_ref[rows, :] for w, a_ref in zip(ws, acc_refs)])
        o_ref[rows, :] = (num / den).astype(o_ref.dtype)
        return carry

    lax.fori_loop(0, seq // blk, merge, 0)


def _attn_prompt(qkv, tail, *, batch, seq, heads):
    e = qkv.shape[1] // (3 * heads)
    assert e == LANES and tail.shape[0] == qkv.shape[0] - batch * seq and tail.shape[0] <= seq
    for window, dil in C_PATTERNS:
        assert window // dil <= C_BLOCK and (seq // dil) % C_BLOCK == 0
    spec = pl.BlockSpec((seq, e), lambda b, h: (b, h))
    part = lambda c: pl.BlockSpec((seq, e), lambda b, h: (b, c * heads + h))
    nstats = 3 * len(C_PATTERNS)
    return pl.pallas_call(
        functools.partial(_attn_prompt_kernel, batch=batch, patterns=C_PATTERNS, blk=C_BLOCK, scale=e ** -0.5),
        grid=(batch + 1, heads),
        in_specs=[part(0), part(1), part(2), pl.BlockSpec((tail.shape[0], e), lambda b, h: (0, h))],
        out_specs=spec,
        out_shape=jax.ShapeDtypeStruct((qkv.shape[0], heads * e), BF16),
        scratch_shapes=(
            [pltpu.VMEM((C_BLOCK, C_BLOCK), F32), pltpu.VMEM((C_BLOCK, 2 * C_BLOCK), F32)]
            + [pltpu.VMEM((seq, e), F32) for _ in range(nstats)]
        ),
        compiler_params=_params(("parallel", "parallel")),
    )(qkv, qkv, qkv, tail)


def _attn_sample_kernel(q_ref, kn_ref, vn_ref, *refs, npat, scale):
    kc_refs, vc_refs, o_ref = refs[:npat], refs[npat:2 * npat], refs[2 * npat]
    q = q_ref[...]
    s_new = jnp.sum(q * kn_ref[...], axis=-1, keepdims=True) * scale
    scores = [jnp.sum(kc[...] * q[None], axis=-1, keepdims=True) * scale for kc in kc_refs]
    top = s_new
    for s in scores:
        top = jnp.maximum(top, jnp.max(s, axis=0))
    w_new = npat * jnp.exp(s_new - top)
    den = w_new
    num = w_new * vn_ref[...]
    for s, vc in zip(scores, vc_refs):
        p = jnp.exp(s - top[None])
        den = den + jnp.sum(p, axis=0)
        num = num + jnp.sum(p * vc[...], axis=0)
    o_ref[...] = num / den


def _attn_sample(q_s, k_s, v_s, cache_k, cache_v, layer, *, head_group=8):
    n, heads, e = q_s.shape
    hist = cache_k.shape[2]
    new = pl.BlockSpec((None, head_group, e), lambda i, g: (i, g, 0))
    views, specs = [], []
    for window, dil in C_PATTERNS:
        reach = window // dil
        assert window <= hist and hist % dil == 0 and (hist // dil) % reach == 0
        last = hist // dil // reach - 1
        views.append((cache_k.shape[0], n, hist // dil, dil, heads, e))
        specs.append(pl.BlockSpec((None, None, reach, None, head_group, e),
                                  lambda i, g, last=last: (layer, i, last, 0, g, 0)))
    return pl.pallas_call(
        functools.partial(_attn_sample_kernel, npat=len(C_PATTERNS), scale=e ** -0.5),
        grid=(n, heads // head_group),
        in_specs=[new, new, new] + specs + specs,
        out_specs=new,
        out_shape=jax.ShapeDtypeStruct((n, heads, e), F32),
        compiler_params=_params(("parallel", "parallel")),
    )(q_s, k_s, v_s, *[cache_k.reshape(v) for v in views], *[cache_v.reshape(v) for v in views])


def _cache_set_last_kernel(new_ref, shifted_ref, o_ref):
    del shifted_ref
    o_ref[...] = new_ref[...]


def _cache_set_last(shifted, new):
    n, hist, heads, e = shifted.shape
    return pl.pallas_call(
        _cache_set_last_kernel,
        grid=(n,),
        in_specs=[pl.BlockSpec((None, 1, heads, e), lambda i: (i, 0, 0, 0)), pl.BlockSpec(memory_space=pl.ANY)],
        out_specs=pl.BlockSpec((None, 1, heads, e), lambda i: (i, hist - 1, 0, 0)),
        out_shape=jax.ShapeDtypeStruct(shifted.shape, shifted.dtype),
        input_output_aliases={1: 0},
        compiler_params=_params(("parallel",)),
    )(new, shifted)


def kernel(x_prompt, x_sample, state_a_conv, cache_c_k, cache_c_v, ln_mix_even, w_in_even, a_conv_w,
           a_conv_b, a_norm_g, a_norm_b, b_norm_g, b_norm_b, b_spatial_w, b_spatial_b, w_out_even,
           ln_mix_odd, w_qkv_odd, w_o_odd, ln_ffn, w_gate_up, w_down, ln_final):
    batch, seq, d = x_prompt.shape
    nsamp = x_sample.shape[0] * x_sample.shape[1]
    depth = ln_ffn.shape[0]
    np_rows = batch * seq
    rows_all = pl.cdiv(np_rows + nsamp, ROW_TILE) * ROW_TILE
    tail = rows_all - np_rows
    assert rows_all % NORM_ROW_TILE == 0 and np_rows % RMS_TILE_OUT == 0 and np_rows % nsamp == 0
    width = a_conv_w.shape[-1]
    ffn = w_down.shape[1]
    e = d // C_HEADS

    def tail_rows(s):
        return jnp.concatenate([s, jnp.zeros((tail - nsamp, s.shape[1]), s.dtype)], axis=0)

    vec3 = lambda t: t.reshape(t.shape[0], 1, t.shape[-1])
    ln_mix = (vec3(ln_mix_even), vec3(ln_mix_odd))

    def mix_gain(i):
        return ln_mix[i % 2], i // 2

    x, xb, rstd = _stack_rows(x_prompt.reshape(np_rows, d), tail_rows(x_sample.reshape(nsamp, d)), *mix_gain(0))

    a_hist_p, a_hist_s, b_v_s = [], [], []
    c_k_p, c_v_p = [], []
    rides = [(c, j) for j in range(cache_c_k.shape[0]) for c in (cache_c_k, cache_c_v)]
    assert len(rides) <= depth
    shifted, new_rows = [], []
    for i in range(depth):
        j = i // 2
        if i % 2 == 0:
            z = _matmul([xb], w_in_even, j, n=w_in_even.shape[-1], tm=ROW_TILE, rstd=rstd)
            s_w0 = jnp.repeat(b_spatial_w[:, :, 0, 0], width // B_GROUPS, axis=-1)
            s_b0 = jnp.repeat(b_spatial_b[:, :, 0], width // B_GROUPS, axis=-1)
            y_s, a_in_s, v_s = _mix_sample(
                z, jnp.swapaxes(state_a_conv[j], 0, 1), a_conv_w, vec3(a_conv_b), vec3(a_norm_g),
                vec3(a_norm_b), vec3(b_norm_g), vec3(b_norm_b), vec3(s_w0), vec3(s_b0), j,
                row0=np_rows, n=nsamp, width=width)
            a, hist_p = _mixa_prompt(z, y_s, a_conv_w, vec3(a_conv_b), vec3(a_norm_g), vec3(a_norm_b), j,
                                     batch=batch, seq=seq, width=width)
            b = _mixb_prompt(z, y_s, vec3(b_norm_g), vec3(b_norm_b), b_spatial_w,
                             jnp.swapaxes(b_spatial_b, 1, 2), j, rows=np_rows, width=width)
            x, xb, rstd = _matmul([a, b], w_out_even, j, n=d, tm=NORM_ROW_TILE, res=x, next_gain=(vec3(ln_ffn), i))
            a_hist_p.append(hist_p)
            a_hist_s.append(jnp.concatenate([state_a_conv[j][:, 1:], a_in_s[:, None, :]], axis=1))
            b_v_s.append(v_s.reshape(x_sample.shape[0], x_sample.shape[1], width))
        else:
            qkv = _matmul([xb], w_qkv_odd, j, n=3 * d, tm=ROW_TILE, rstd=rstd)
            qkv_s = qkv[np_rows:np_rows + nsamp].reshape(nsamp, 3, C_HEADS, e)
            o_s = _attn_sample(qkv_s[:, 0], qkv_s[:, 1], qkv_s[:, 2], cache_c_k, cache_c_v, j)
            o = _attn_prompt(qkv, tail_rows(o_s.reshape(nsamp, d)), batch=batch, seq=seq, heads=C_HEADS)
            x, xb, rstd = _matmul([o], w_o_odd, j, n=d, tm=NORM_ROW_TILE, res=x, next_gain=(vec3(ln_ffn), i))
            keep = min(C_PATTERNS[-1][0], seq)
            c_k_p.append(qkv[:np_rows, d:2 * d].reshape(batch, seq, C_HEADS, e)[:, seq - keep:])
            c_v_p.append(qkv[:np_rows, 2 * d:].reshape(batch, seq, C_HEADS, e)[:, seq - keep:])
            new_rows += [qkv_s[:, 1:2], qkv_s[:, 2:3]]
        hid, bulk = _matmul_gated(xb, w_gate_up, i, rstd, half=ffn, ride=rides[i] if i < len(rides) else None)
        shifted.append(bulk)
        down = functools.partial(_matmul, [hid], w_down, i, n=d, tm=DOWN_ROW_TILE, tn=GATE_TILE, nk=2, res=x)
        if i + 1 < depth:
            x, xb, rstd = down(next_gain=mix_gain(i + 1))
        else:
            x = down()

    c_s = [_cache_set_last(bulk, new) for bulk, new in zip(shifted, new_rows)]
    c_k_s, c_v_s = c_s[0::2], c_s[1::2]
    g_final = ln_final.reshape(1, 1, d)
    y_prompt = _rmsnorm(x, g_final, 0, F32, rows=np_rows, row_tile=RMS_TILE_OUT)
    y_sample = _rmsnorm(x, g_final, 0, F32, rows=nsamp, row_tile=nsamp, first_block=np_rows // nsamp)
    return (y_prompt.reshape(batch, seq, d), y_sample.reshape(x_sample.shape),
            jnp.stack(a_hist_p), jnp.stack(a_hist_s), jnp.stack(b_v_s),
            jnp.stack(c_k_p), jnp.stack(c_v_p), jnp.stack(c_k_s), jnp.stack(c_v_s))
```

```python
import functools

import jax
import jax.numpy as jnp
from jax import lax
from jax.experimental import pallas as pl
from jax.experimental.pallas import tpu as pltpu

F32 = jnp.float32
BF16 = jnp.bfloat16

LANES = 128
VMEM_BYTES = 64 * 1024 * 1024
VMEM_LIMIT = VMEM_BYTES - 8 * 1024 * 1024

RMS_EPS = 1e-6
LN_EPS = 1e-5

A_GROUPS = 16
B_GROUPS = 16
B_CHUNK = 128
C_HEADS = 32
C_PATTERNS = ((128, 1), (512, 4), (2048, 16))
C_BLOCK = 128

ROW_TILE = 2080
DOWN_ROW_TILE = 1040
N_TILE = 512
GATE_TILE = 256
NORM_ROW_TILE = 1664
RMS_TILE_OUT = 256


def _params(semantics):
    return pltpu.CompilerParams(dimension_semantics=semantics, vmem_limit_bytes=VMEM_LIMIT)


def _rms_kernel(x_ref, g_ref, o_ref):
    x = x_ref[...]
    y = x * lax.rsqrt(jnp.mean(x * x, axis=-1, keepdims=True) + RMS_EPS)
    o_ref[...] = (y * g_ref[...]).astype(o_ref.dtype)


def _rmsnorm(x, gains, layer, out_dtype, *, rows, row_tile, first_block=0):
    d = x.shape[1]
    return pl.pallas_call(
        _rms_kernel,
        grid=(rows // row_tile,),
        in_specs=[
            pl.BlockSpec((row_tile, d), lambda i: (i + first_block, 0)),
            pl.BlockSpec((None, 1, d), lambda i: (layer, 0, 0)),
        ],
        out_specs=pl.BlockSpec((row_tile, d), lambda i: (i, 0)),
        out_shape=jax.ShapeDtypeStruct((rows, d), out_dtype),
        compiler_params=_params(("parallel",)),
    )(x, gains)


def _stack_kernel(x_ref, tail_ref, gain_ref, o_ref, ob_ref, rstd_ref, *, nblk):
    def emit(x):
        rows = x.shape[0]
        o_ref[0:rows, :] = x
        ob_ref[0:rows, :] = (x * gain_ref[...]).astype(ob_ref.dtype)
        rstd = lax.rsqrt(jnp.mean(x * x, axis=-1, keepdims=True) + RMS_EPS)
        rstd_ref[0:rows, :] = jnp.broadcast_to(rstd, (rows, LANES))

    @pl.when(pl.program_id(0) < nblk)
    def _():
        emit(x_ref[...])

    @pl.when(pl.program_id(0) == nblk)
    def _():
        emit(tail_ref[...])


def _stack_rows(x_prompt, tail, gains, gain_idx, *, row_tile=RMS_TILE_OUT):
    p, d = x_prompt.shape
    t = tail.shape[0]
    assert p % row_tile == 0 and t <= row_tile
    nblk = p // row_tile
    rows = lambda width: pl.BlockSpec((row_tile, width), lambda i: (i, 0))
    return pl.pallas_call(
        functools.partial(_stack_kernel, nblk=nblk),
        grid=(nblk + 1,),
        in_specs=[pl.BlockSpec((row_tile, d), lambda i: (jnp.minimum(i, nblk - 1), 0)),
                  pl.BlockSpec((t, d), lambda i: (0, 0)),
                  pl.BlockSpec((None, 1, d), lambda i: (gain_idx, 0, 0))],
        out_specs=[rows(d), rows(d), rows(LANES)],
        out_shape=[jax.ShapeDtypeStruct((p + t, d), F32), jax.ShapeDtypeStruct((p + t, d), BF16),
                   jax.ShapeDtypeStruct((p + t, LANES), F32)],
        compiler_params=_params(("parallel",)),
    )(x_prompt, tail, gains)


def _mm_kernel(*refs, a_widths, nk, tk, has_rstd, has_res, emit_norm):
    it = iter(refs)
    a_refs = [next(it) for _ in a_widths]
    b_ref = next(it)
    rstd_ref = next(it) if has_rstd else None
    res_ref = next(it) if has_res else None
    gain_ref = next(it) if emit_norm else None
    o_ref = next(it)
    ob_ref, rstd_out_ref = (next(it), next(it)) if emit_norm else (None, None)
    acc_ref = next(it) if nk > 1 else None
    ss_ref = next(it) if emit_norm else None
    j, nj = pl.program_id(1), pl.num_programs(1)

    def finish(acc):
        if has_rstd:
            acc = acc * jnp.tile(rstd_ref[...], (1, acc.shape[1] // LANES))
        if has_res:
            acc = acc + res_ref[...]
        o_ref[...] = acc.astype(o_ref.dtype)
        if emit_norm:
            ob_ref[...] = (acc * gain_ref[...]).astype(ob_ref.dtype)
            sq = acc * acc
            part = sq[:, 0:LANES]
            for c in range(1, acc.shape[1] // LANES):
                part = part + sq[:, c * LANES:(c + 1) * LANES]

            total = jnp.where(j == 0, part, ss_ref[...] + part)
            ss_ref[...] = total
            mean_sq = jnp.sum(total, axis=-1, keepdims=True) / (nj * acc.shape[1])
            rstd_out_ref[...] = jnp.broadcast_to(lax.rsqrt(mean_sq + RMS_EPS), rstd_out_ref.shape)

    if nk == 1:
        acc, k0 = None, 0
        for a_ref, kw in zip(a_refs, a_widths):
            part = jnp.dot(a_ref[...], b_ref[k0:k0 + kw, :].astype(BF16), preferred_element_type=F32)
            acc = part if acc is None else acc + part
            k0 += kw
        finish(acc)
        return

    (a_ref,) = a_refs
    k = pl.program_id(2)
    for kk in range(nk):
        @pl.when(k == kk)
        def _(kk=kk):
            part = jnp.dot(a_ref[:, kk * tk:(kk + 1) * tk], b_ref[...].astype(BF16),
                           preferred_element_type=F32)
            if kk == 0:
                acc_ref[...] = part
            elif kk < nk - 1:
                acc_ref[...] += part
            else:
                finish(acc_ref[...] + part)


def _matmul(a_parts, w, layer, *, n, tm, tn=N_TILE, nk=1, rstd=None, res=None, next_gain=None):
    emit_norm = next_gain is not None
    m = a_parts[0].shape[0]
    a_widths = tuple(a.shape[1] for a in a_parts)
    kdim = sum(a_widths)
    assert nk == 1 or len(a_parts) == 1
    assert w.shape[-1] == n or not emit_norm
    tk = kdim // nk
    in_specs = [pl.BlockSpec((tm, kw), lambda i, j, k: (i, 0), pipeline_mode=pl.Buffered(1)) for kw in a_widths]
    in_specs.append(pl.BlockSpec((None, tk, tn), lambda i, j, k: (layer, k, j)))
    args = [*a_parts, w]
    if rstd is not None:
        in_specs.append(pl.BlockSpec((tm, LANES), lambda i, j, k: (i, 0)))
        args.append(rstd)
    if res is not None:
        in_specs.append(pl.BlockSpec((tm, tn), lambda i, j, k: (i, j)))
        args.append(res)
    if emit_norm:
        gains, gain_idx = next_gain
        in_specs.append(pl.BlockSpec((None, 1, tn), lambda i, j, k: (gain_idx, 0, j)))
        args.append(gains)
    tile = pl.BlockSpec((tm, tn), lambda i, j, k: (i, j))
    out_specs, out_shape = [tile], [jax.ShapeDtypeStruct((m, n), F32)]
    scratch = [pltpu.VMEM((tm, tn), F32)] if nk > 1 else []
    if emit_norm:
        out_specs += [tile, pl.BlockSpec((tm, LANES), lambda i, j, k: (i, 0))]
        out_shape += [jax.ShapeDtypeStruct((m, n), BF16), jax.ShapeDtypeStruct((m, LANES), F32)]
        scratch.append(pltpu.VMEM((tm, LANES), F32))
    outs = pl.pallas_call(
        functools.partial(_mm_kernel, a_widths=a_widths, nk=nk, tk=tk, has_rstd=rstd is not None,
                          has_res=res is not None, emit_norm=emit_norm),
        grid=(m // tm, n // tn, nk),
        in_specs=in_specs,
        out_specs=out_specs,
        out_shape=out_shape,
        scratch_shapes=scratch,
        compiler_params=_params(("parallel", "arbitrary", "arbitrary")),
    )(*args)
    return outs if emit_norm else outs[0]


def _mm_gated_kernel(a_ref, bg_ref, bu_ref, rstd_ref, *refs, nj, ride_steps):
    o_ref = refs[-1] if ride_steps == 0 else refs[-2]
    a = a_ref[...]
    rstd = jnp.tile(rstd_ref[...], (1, o_ref.shape[1] // LANES))
    g = jnp.dot(a, bg_ref[...].astype(BF16), preferred_element_type=F32) * rstd
    u = jnp.dot(a, bu_ref[...].astype(BF16), preferred_element_type=F32) * rstd
    o_ref[...] = (g * jax.nn.sigmoid(g) * u).astype(o_ref.dtype)

    if ride_steps:
        old_ref, next_ref, shifted_ref = refs[0], refs[1], refs[-1]
        rows = shifted_ref.shape[0]

        shifted_ref[0:rows - 1] = old_ref[1:rows]
        shifted_ref[rows - 1] = next_ref[0]


def _matmul_gated(a, w, layer, rstd, *, half, tm=ROW_TILE, tn=GATE_TILE, ride=None, ride_rows=128):
    m, kdim = a.shape
    ni, nj = m // tm, half // tn
    in_specs = [
        pl.BlockSpec((tm, kdim), lambda i, j: (i, 0), pipeline_mode=pl.Buffered(1)),
        pl.BlockSpec((None, kdim, tn), lambda i, j: (layer, 0, j)),
        pl.BlockSpec((None, kdim, tn), lambda i, j: (layer, 0, j + nj)),
        pl.BlockSpec((tm, LANES), lambda i, j: (i, 0)),
    ]
    args = [a, w, w, rstd]
    out_specs = [pl.BlockSpec((tm, tn), lambda i, j: (i, j))]
    out_shape = [jax.ShapeDtypeStruct((m, half), BF16)]
    ride_steps = 0
    if ride is not None:
        cache, cache_layer = ride
        _, n, hist, heads, e = cache.shape
        assert hist % ride_rows == 0
        nblk = hist // ride_rows
        ride_steps = n * nblk
        assert ride_steps <= ni * nj

        def seq_blk(i, j):
            step = jnp.minimum(i * nj + j, ride_steps - 1)
            return step // nblk, step % nblk

        def old_map(i, j):
            sq, t = seq_blk(i, j)
            return cache_layer, sq, t, 0, 0

        def next_map(i, j):
            sq, t = seq_blk(i, j)
            return cache_layer, sq, jnp.minimum((t + 1) * ride_rows, hist - 1), 0, 0

        def out_map(i, j):
            sq, t = seq_blk(i, j)
            return sq, t, 0, 0

        in_specs += [pl.BlockSpec((None, None, ride_rows, heads, e), old_map),
                     pl.BlockSpec((None, None, 1, heads, e), next_map)]
        args += [cache, cache]
        out_specs.append(pl.BlockSpec((None, ride_rows, heads, e), out_map))
        out_shape.append(jax.ShapeDtypeStruct(cache.shape[1:], cache.dtype))
    outs = pl.pallas_call(
        functools.partial(_mm_gated_kernel, nj=nj, ride_steps=ride_steps),
        grid=(ni, nj),
        in_specs=in_specs,
        out_specs=out_specs,
        out_shape=out_shape,
        compiler_params=_params(("arbitrary", "arbitrary") if ride_steps else ("parallel", "parallel")),
    )(*args)
    return outs if ride_steps else (outs[0], None)


def _gelu(x):
    return 0.5 * x * (1.0 + lax.erf(x * (2.0 ** -0.5)))


def _layer_norm(x, g, b):
    xc = x - jnp.mean(x, axis=-1, keepdims=True)
    y = xc * lax.rsqrt(jnp.mean(xc * xc, axis=-1, keepdims=True) + LN_EPS)
    return y * g + b


def _silu(x):
    return x * jax.nn.sigmoid(x)


def _mixa_prompt_kernel(val_ref, gate_ref, w_ref, cb_ref, g_ref, b_ref, tail_ref, a_ref, st_ref, ext_ref,
                        *, batch, seq, kw, hist_pad, chunk):
    b = pl.program_id(1)

    @pl.when(b == batch)
    def _():
        a_ref[0:tail_ref.shape[0], :] = tail_ref[...].astype(a_ref.dtype)

    @pl.when(b < batch)
    def _():
        ext_ref[0:hist_pad, :] = jnp.zeros((hist_pad, LANES), F32)
        ext_ref[hist_pad:hist_pad + seq, :] = val_ref[...] * jax.nn.sigmoid(gate_ref[...])
        st_ref[...] = ext_ref[hist_pad + seq - (kw - 1):hist_pad + seq, :]
        off = hist_pad - (kw - 1)
        for c in range(seq // chunk):
            t0 = c * chunk
            acc = w_ref[0:1, :] * ext_ref[t0 + off:t0 + off + chunk, :]
            for k in range(1, kw):
                acc = acc + w_ref[k:k + 1, :] * ext_ref[t0 + off + k:t0 + off + k + chunk, :]
            y = _layer_norm(acc + cb_ref[...], g_ref[...], b_ref[...])
            a_ref[t0:t0 + chunk, :] = _silu(y).astype(a_ref.dtype)


def _mixa_prompt(z, tail, conv_w, conv_b, norm_g, norm_b, layer, *, batch, seq, width):
    kw = conv_w.shape[1]
    nc = width // LANES
    hist_pad = 32
    assert tail.shape[0] == z.shape[0] - batch * seq and tail.shape[0] <= seq
    vec = lambda: pl.BlockSpec((None, 1, LANES), lambda c, b: (layer, 0, c))
    return pl.pallas_call(
        functools.partial(_mixa_prompt_kernel, batch=batch, seq=seq, kw=kw, hist_pad=hist_pad, chunk=128),
        grid=(nc, batch + 1),
        in_specs=[
            pl.BlockSpec((seq, LANES), lambda c, b: (b, c)),
            pl.BlockSpec((seq, LANES), lambda c, b: (b, c + nc)),
            pl.BlockSpec((None, kw, LANES), lambda c, b: (layer, 0, c)),
            vec(), vec(), vec(),
            pl.BlockSpec((tail.shape[0], LANES), lambda c, b: (0, c)),
        ],
        out_specs=[
            pl.BlockSpec((seq, LANES), lambda c, b: (b, c)),
            pl.BlockSpec((None, kw - 1, LANES), lambda c, b: (jnp.minimum(b, batch - 1), 0, c)),
        ],
        out_shape=[
            jax.ShapeDtypeStruct((z.shape[0], width), BF16),
            jax.ShapeDtypeStruct((batch, kw - 1, width), F32),
        ],
        scratch_shapes=[pltpu.VMEM((hist_pad + seq, LANES), F32)],
        compiler_params=_params(("parallel", "arbitrary")),
    )(z, z, conv_w, conv_b, norm_g, norm_b, tail)


def _mixb_prompt_kernel(u_ref, v_ref, ng_ref, nb_ref, ws_ref, bst_ref, tail_ref, o_ref, *, groups, nchunks):
    step = pl.program_id(0)

    @pl.when(step == nchunks)
    def _():
        o_ref[...] = tail_ref[...].astype(o_ref.dtype)

    @pl.when(step < nchunks)
    def _():
        v = _layer_norm(_gelu(v_ref[...]), ng_ref[...], nb_ref[...]).astype(BF16)
        n = ws_ref.shape[-1]
        causal = lax.broadcasted_iota(jnp.int32, (n, n), 1) <= lax.broadcasted_iota(jnp.int32, (n, n), 0)
        for g in range(groups):
            cols = slice(g * LANES, (g + 1) * LANES)
            w_m = jnp.where(causal, ws_ref[g], 0.0).astype(BF16)
            s = jnp.dot(w_m, v[:, cols], preferred_element_type=F32) + bst_ref[:, g:g + 1]
            o_ref[:, cols] = (_gelu(u_ref[:, cols]) * s).astype(o_ref.dtype)


def _mixb_prompt(z, tail, norm_g, norm_b, spatial_w, spatial_bt, layer, *, rows, width):
    groups, chunk = spatial_w.shape[1], spatial_w.shape[2]
    assert tail.shape[0] == z.shape[0] - rows == chunk
    vec = lambda: pl.BlockSpec((None, 1, width), lambda s: (layer, 0, 0))
    return pl.pallas_call(
        functools.partial(_mixb_prompt_kernel, groups=groups, nchunks=rows // chunk),
        grid=(rows // chunk + 1,),
        in_specs=[
            pl.BlockSpec((chunk, width), lambda s: (s, 2)),
            pl.BlockSpec((chunk, width), lambda s: (s, 3)),
            vec(), vec(),
            pl.BlockSpec((None, groups, chunk, chunk), lambda s: (layer, 0, 0, 0)),
            pl.BlockSpec((None, chunk, groups), lambda s: (layer, 0, 0)),
            pl.BlockSpec((chunk, width), lambda s: (0, 1)),
        ],
        out_specs=pl.BlockSpec((chunk, width), lambda s: (s, 0)),
        out_shape=jax.ShapeDtypeStruct((z.shape[0], width), BF16),
        compiler_params=_params(("parallel",)),
    )(z, z, norm_g, norm_b, spatial_w, spatial_bt, tail)


def _mix_sample_kernel(val_ref, gate_ref, u_ref, v_ref, st_ref, cw_ref, cb_ref, ag_ref, ab_ref,
                       bg_ref, bb_ref, sw_ref, sb_ref, y_ref, ain_ref, vout_ref, *, kw, groups, width):
    n = val_ref.shape[0]
    y_ref[...] = jnp.zeros(y_ref.shape, y_ref.dtype)
    a_in = val_ref[...] * jax.nn.sigmoid(gate_ref[...])
    ain_ref[...] = a_in
    acc = cw_ref[0:1, :] * st_ref[0]
    for k in range(1, kw - 1):
        acc = acc + cw_ref[k:k + 1, :] * st_ref[k]
    acc = acc + cw_ref[kw - 1:kw, :] * a_in + cb_ref[...]
    for g in range(groups):
        cols = slice(g * LANES, (g + 1) * LANES)
        y = _layer_norm(acc[:, cols], ag_ref[:, cols], ab_ref[:, cols])
        y_ref[0:n, cols] = _silu(y)
    v = _layer_norm(_gelu(v_ref[...]), bg_ref[...], bb_ref[...])
    vout_ref[...] = v
    y_ref[0:n, width:2 * width] = _gelu(u_ref[...]) * (sw_ref[...] * v + sb_ref[...])


def _mix_sample(z, state_t, conv_w, conv_b, a_g, a_b, b_g, b_b, s_w0, s_b0, layer, *, row0, n, width):
    tail = z.shape[0] - row0
    kw = conv_w.shape[1]
    rb = row0 // n
    zspec = lambda c: pl.BlockSpec((n, width), lambda i: (rb, c))
    vec = lambda: pl.BlockSpec((None, 1, width), lambda i: (layer, 0, 0))
    return pl.pallas_call(
        functools.partial(_mix_sample_kernel, kw=kw, groups=A_GROUPS, width=width),
        grid=(1,),
        in_specs=[
            zspec(0), zspec(1), zspec(2), zspec(3),
            pl.BlockSpec((kw - 1, n, width), lambda i: (0, 0, 0)),
            pl.BlockSpec((None, kw, width), lambda i: (layer, 0, 0)),
            vec(), vec(), vec(), vec(), vec(), vec(), vec(),
        ],
        out_specs=[
            pl.BlockSpec((tail, 2 * width), lambda i: (0, 0)),
            pl.BlockSpec((n, width), lambda i: (0, 0)),
            pl.BlockSpec((n, width), lambda i: (0, 0)),
        ],
        out_shape=[
            jax.ShapeDtypeStruct((tail, 2 * width), F32),
            jax.ShapeDtypeStruct((n, width), F32),
            jax.ShapeDtypeStruct((n, width), F32),
        ],
        compiler_params=_params(("arbitrary",)),
    )(z, z, z, z, state_t, conv_w, conv_b, a_g, a_b, b_g, b_b, s_w0, s_b0)


def _rows(start, size, stride):
    return pl.ds(start, size) if stride == 1 else pl.ds(start, size, stride=stride)


LOG2E = 1.4426950408889634


def _attn_prompt_kernel(q_ref, k_ref, v_ref, tail_ref, o_ref, bias1_ref, bias2_ref, *stats,
                        batch, patterns, blk, scale):
    @pl.when(pl.program_id(0) == batch)
    def _():
        o_ref[0:tail_ref.shape[0], :] = tail_ref[...].astype(o_ref.dtype)

    @pl.when(pl.program_id(0) < batch)
    def _():
        _attn_prompt_sequence(q_ref, k_ref, v_ref, o_ref, bias1_ref, bias2_ref, stats, patterns, blk, scale)


def _attn_prompt_sequence(q_ref, k_ref, v_ref, o_ref, bias1_ref, bias2_ref, stats, patterns, blk, scale):
    seq, e = q_ref.shape
    npat = len(patterns)
    acc_refs, m_refs, l_refs = stats[:npat], stats[npat:2 * npat], stats[2 * npat:]
    c = scale * LOG2E

    def distance(nk):
        qi = lax.broadcasted_iota(jnp.int32, (blk, nk), 0)
        kj = lax.broadcasted_iota(jnp.int32, (blk, nk), 1)
        return (nk - blk) + qi - kj

    def attend(q_rows, k_rows, bias):
        qb = q_ref[q_rows, :].astype(BF16)
        kb = k_ref[k_rows, :].astype(BF16)
        vb = v_ref[k_rows, :].astype(BF16)
        s = lax.dot_general(qb, kb, (((1,), (1,)), ((), ())), preferred_element_type=F32) + bias
        m = jnp.max(s, axis=-1, keepdims=True)
        p = jnp.exp2((s - m) * c).astype(BF16)
        ov = jnp.dot(p, jnp.concatenate([vb, jnp.ones_like(vb)], axis=1), preferred_element_type=F32)
        return ov[:, :e], ov[:, e:], jnp.broadcast_to(m, (blk, e))

    for p, (window, dil) in enumerate(patterns):
        reach = window // dil
        nblk = seq // dil // blk
        for bias_ref in (bias1_ref, bias2_ref):
            dist = distance(bias_ref.shape[1])
            bias_ref[...] = jnp.where((dist >= 0) & (dist <= reach), 0.0, -jnp.inf)
        for r in range(dil):
            for b in range(nblk):
                q_rows = _rows(r + dil * blk * b, blk, dil)
                if b == 0:
                    k_rows, bias = q_rows, bias1_ref[...]
                else:
                    k_rows, bias = _rows(r + dil * blk * (b - 1), 2 * blk, dil), bias2_ref[...]
                acc_refs[p][q_rows, :], l_refs[p][q_rows, :], m_refs[p][q_rows, :] = attend(q_rows, k_rows, bias)

    def merge(t, carry):
        rows = pl.ds(pl.multiple_of(t * blk, blk), blk)
        ms = [m_ref[rows, :] for m_ref in m_refs]
        top = functools.reduce(jnp.maximum, ms)
        ws = [jnp.exp2((m - top) * c) for m in ms]
        den = functools.reduce(lambda x, y: x + y, [w * l_ref[rows, :] for w, l_ref in zip(ws, l_refs)])
        num = functools.reduce(lambda x, y: x + y, [w * a_ref[rows, :] for w, a_ref in zip(ws, acc_refs)])
        o_ref[rows, :] = (num / den).astype(o_ref.dtype)
        return carry

    lax.fori_loop(0, seq // blk, merge, 0, unroll=2)


def _attn_prompt(qkv, tail, *, batch, seq, heads):
    e = qkv.shape[1] // (3 * heads)
    assert e == LANES and tail.shape[0] == qkv.shape[0] - batch * seq and tail.shape[0] <= seq
    for window, dil in C_PATTERNS:
        assert window // dil <= C_BLOCK and (seq // dil) % C_BLOCK == 0
    spec = pl.BlockSpec((seq, e), lambda b, h: (b, h))
    part = lambda c: pl.BlockSpec((seq, e), lambda b, h: (b, c * heads + h))
    nstats = 3 * len(C_PATTERNS)
    return pl.pallas_call(
        functools.partial(_attn_prompt_kernel, batch=batch, patterns=C_PATTERNS, blk=C_BLOCK, scale=e ** -0.5),
        grid=(batch + 1, heads),
        in_specs=[part(0), part(1), part(2), pl.BlockSpec((tail.shape[0], e), lambda b, h: (0, h))],
        out_specs=spec,
        out_shape=jax.ShapeDtypeStruct((qkv.shape[0], heads * e), BF16),
        scratch_shapes=(
            [pltpu.VMEM((C_BLOCK, C_BLOCK), F32), pltpu.VMEM((C_BLOCK, 2 * C_BLOCK), F32)]
            + [pltpu.VMEM((seq, e), F32) for _ in range(nstats)]
        ),
        compiler_params=_params(("parallel", "parallel")),
    )(qkv, qkv, qkv, tail)


def _attn_sample_kernel(q_ref, kn_ref, vn_ref, *refs, npat, scale):
    kc_refs, vc_refs, o_ref = refs[:npat], refs[npat:2 * npat], refs[2 * npat]
    q = q_ref[...]
    s_new = jnp.sum(q * kn_ref[...], axis=-1, keepdims=True) * scale
    scores = [jnp.sum(kc[...] * q[None], axis=-1, keepdims=True) * scale for kc in kc_refs]
    top = s_new
    for s in scores:
        top = jnp.maximum(top, jnp.max(s, axis=0))
    w_new = npat * jnp.exp(s_new - top)
    den = w_new
    num = w_new * vn_ref[...]
    for s, vc in zip(scores, vc_refs):
        p = jnp.exp(s - top[None])
        den = den + jnp.sum(p, axis=0)
        num = num + jnp.sum(p * vc[...], axis=0)
    o_ref[...] = num / den


def _attn_sample(q_s, k_s, v_s, cache_k, cache_v, layer, *, head_group=32):
    n, heads, e = q_s.shape
    hist = cache_k.shape[2]
    new = pl.BlockSpec((None, head_group, e), lambda i, g: (i, g, 0))
    views, specs = [], []
    for window, dil in C_PATTERNS:
        reach = window // dil
        assert window <= hist and hist % dil == 0 and (hist // dil) % reach == 0
        last = hist // dil // reach - 1
        views.append((cache_k.shape[0], n, hist // dil, dil, heads, e))
        specs.append(pl.BlockSpec((None, None, reach, None, head_group, e),
                                  lambda i, g, last=last: (layer, i, last, 0, g, 0)))
    return pl.pallas_call(
        functools.partial(_attn_sample_kernel, npat=len(C_PATTERNS), scale=e ** -0.5),
        grid=(n, heads // head_group),
        in_specs=[new, new, new] + specs + specs,
        out_specs=new,
        out_shape=jax.ShapeDtypeStruct((n, heads, e), F32),
        compiler_params=_params(("parallel", "parallel")),
    )(q_s, k_s, v_s, *[cache_k.reshape(v) for v in views], *[cache_v.reshape(v) for v in views])


def _cache_set_last_kernel(new_ref, shifted_ref, o_ref):
    del shifted_ref
    o_ref[...] = new_ref[...]


def _cache_set_last(shifted, new):
    n, hist, heads, e = shifted.shape
    return pl.pallas_call(
        _cache_set_last_kernel,
        grid=(n,),
        in_specs=[pl.BlockSpec((None, 1, heads, e), lambda i: (i, 0, 0, 0)), pl.BlockSpec(memory_space=pl.ANY)],
        out_specs=pl.BlockSpec((None, 1, heads, e), lambda i: (i, hist - 1, 0, 0)),
        out_shape=jax.ShapeDtypeStruct(shifted.shape, shifted.dtype),
        input_output_aliases={1: 0},
        compiler_params=_params(("parallel",)),
    )(new, shifted)


def kernel(x_prompt, x_sample, state_a_conv, cache_c_k, cache_c_v, ln_mix_even, w_in_even, a_conv_w,
           a_conv_b, a_norm_g, a_norm_b, b_norm_g, b_norm_b, b_spatial_w, b_spatial_b, w_out_even,
           ln_mix_odd, w_qkv_odd, w_o_odd, ln_ffn, w_gate_up, w_down, ln_final):
    batch, seq, d = x_prompt.shape
    nsamp = x_sample.shape[0] * x_sample.shape[1]
    depth = ln_ffn.shape[0]
    np_rows = batch * seq
    rows_all = pl.cdiv(np_rows + nsamp, ROW_TILE) * ROW_TILE
    tail = rows_all - np_rows
    assert rows_all % NORM_ROW_TILE == 0 and np_rows % RMS_TILE_OUT == 0 and np_rows % nsamp == 0
    width = a_conv_w.shape[-1]
    ffn = w_down.shape[1]
    e = d // C_HEADS

    def tail_rows(s):
        return jnp.concatenate([s, jnp.zeros((tail - nsamp, s.shape[1]), s.dtype)], axis=0)

    vec3 = lambda t: t.reshape(t.shape[0], 1, t.shape[-1])
    ln_mix = (vec3(ln_mix_even), vec3(ln_mix_odd))

    def mix_gain(i):
        return ln_mix[i % 2], i // 2

    x, xb, rstd = _stack_rows(x_prompt.reshape(np_rows, d), tail_rows(x_sample.reshape(nsamp, d)), *mix_gain(0))

    a_hist_p, a_hist_s, b_v_s = [], [], []
    c_k_p, c_v_p = [], []
    rides = [(c, j) for j in range(cache_c_k.shape[0]) for c in (cache_c_k, cache_c_v)]
    assert len(rides) <= depth
    shifted, new_rows = [], []
    for i in range(depth):
        j = i // 2
        if i % 2 == 0:
            z = _matmul([xb], w_in_even, j, n=w_in_even.shape[-1], tm=ROW_TILE, rstd=rstd)
            s_w0 = jnp.repeat(b_spatial_w[:, :, 0, 0], width // B_GROUPS, axis=-1)
            s_b0 = jnp.repeat(b_spatial_b[:, :, 0], width // B_GROUPS, axis=-1)
            y_s, a_in_s, v_s = _mix_sample(
                z, jnp.swapaxes(state_a_conv[j], 0, 1), a_conv_w, vec3(a_conv_b), vec3(a_norm_g),
                vec3(a_norm_b), vec3(b_norm_g), vec3(b_norm_b), vec3(s_w0), vec3(s_b0), j,
                row0=np_rows, n=nsamp, width=width)
            a, hist_p = _mixa_prompt(z, y_s, a_conv_w, vec3(a_conv_b), vec3(a_norm_g), vec3(a_norm_b), j,
                                     batch=batch, seq=seq, width=width)
            b = _mixb_prompt(z, y_s, vec3(b_norm_g), vec3(b_norm_b), b_spatial_w,
                             jnp.swapaxes(b_spatial_b, 1, 2), j, rows=np_rows, width=width)
            x, xb, rstd = _matmul([a, b], w_out_even, j, n=d, tm=NORM_ROW_TILE, res=x, next_gain=(vec3(ln_ffn), i))
            a_hist_p.append(hist_p)
            a_hist_s.append(jnp.concatenate([state_a_conv[j][:, 1:], a_in_s[:, None, :]], axis=1))
            b_v_s.append(v_s.reshape(x_sample.shape[0], x_sample.shape[1], width))
        else:
            qkv = _matmul([xb], w_qkv_odd, j, n=3 * d, tm=ROW_TILE, rstd=rstd)
            qkv_s = qkv[np_rows:np_rows + nsamp].reshape(nsamp, 3, C_HEADS, e)
            o_s = _attn_sample(qkv_s[:, 0], qkv_s[:, 1], qkv_s[:, 2], cache_c_k, cache_c_v, j)
            o = _attn_prompt(qkv, tail_rows(o_s.reshape(nsamp, d)), batch=batch, seq=seq, heads=C_HEADS)
            x, xb, rstd = _matmul([o], w_o_odd, j, n=d, tm=NORM_ROW_TILE, res=x, next_gain=(vec3(ln_ffn), i))
            keep = min(C_PATTERNS[-1][0], seq)
            c_k_p.append(qkv[:np_rows, d:2 * d].reshape(batch, seq, C_HEADS, e)[:, seq - keep:])
            c_v_p.append(qkv[:np_rows, 2 * d:].reshape(batch, seq, C_HEADS, e)[:, seq - keep:])
            new_rows += [qkv_s[:, 1:2], qkv_s[:, 2:3]]
        hid, bulk = _matmul_gated(xb, w_gate_up, i, rstd, half=ffn, ride=rides[i] if i < len(rides) else None)
        shifted.append(bulk)
        down = functools.partial(_matmul, [hid], w_down, i, n=d, tm=DOWN_ROW_TILE, tn=GATE_TILE, nk=2, res=x)
        if i + 1 < depth:
            x, xb, rstd = down(next_gain=mix_gain(i + 1))
        else:
            x = down()

    c_s = [_cache_set_last(bulk, new) for bulk, new in zip(shifted, new_rows)]
    c_k_s, c_v_s = c_s[0::2], c_s[1::2]
    g_final = ln_final.reshape(1, 1, d)
    y_prompt = _rmsnorm(x, g_final, 0, F32, rows=np_rows, row_tile=RMS_TILE_OUT)
    y_sample = _rmsnorm(x, g_final, 0, F32, rows=nsamp, row_tile=nsamp, first_block=np_rows // nsamp)
    return (y_prompt.reshape(batch, seq, d), y_sample.reshape(x_sample.shape),
            jnp.stack(a_hist_p), jnp.stack(a_hist_s), jnp.stack(b_v_s),
            jnp.stack(c_k_p), jnp.stack(c_v_p), jnp.stack(c_k_s), jnp.stack(c_v_s))
```

```python
import functools

import jax
import jax.numpy as jnp
from jax import lax
from jax.experimental import pallas as pl
from jax.experimental.pallas import tpu as pltpu

F32 = jnp.float32
BF16 = jnp.bfloat16

LANES = 128
VMEM_BYTES = 64 * 1024 * 1024
VMEM_LIMIT = VMEM_BYTES - 8 * 1024 * 1024

RMS_EPS = 1e-6
LN_EPS = 1e-5

A_GROUPS = 16
B_GROUPS = 16
B_CHUNK = 128
C_HEADS = 32
C_PATTERNS = ((128, 1), (512, 4), (2048, 16))
C_BLOCK = 128

ROW_TILE = 2080
DOWN_ROW_TILE = 1040
N_TILE = 512
GATE_TILE = 256
NORM_ROW_TILE = 1664
RMS_TILE_OUT = 256


def _params(semantics):
    return pltpu.CompilerParams(dimension_semantics=semantics, vmem_limit_bytes=VMEM_LIMIT)


def _rms_kernel(x_ref, g_ref, o_ref):
    x = x_ref[...]
    y = x * lax.rsqrt(jnp.mean(x * x, axis=-1, keepdims=True) + RMS_EPS)
    o_ref[...] = (y * g_ref[...]).astype(o_ref.dtype)


def _rmsnorm(x, gains, layer, out_dtype, *, rows, row_tile, first_block=0):
    d = x.shape[1]
    return pl.pallas_call(
        _rms_kernel,
        grid=(rows // row_tile,),
        in_specs=[
            pl.BlockSpec((row_tile, d), lambda i: (i + first_block, 0)),
            pl.BlockSpec((None, 1, d), lambda i: (layer, 0, 0)),
        ],
        out_specs=pl.BlockSpec((row_tile, d), lambda i: (i, 0)),
        out_shape=jax.ShapeDtypeStruct((rows, d), out_dtype),
        compiler_params=_params(("parallel",)),
    )(x, gains)


def _stack_kernel(x_ref, tail_ref, gain_ref, o_ref, ob_ref, rstd_ref, *, nblk):
    def emit(x):
        rows = x.shape[0]
        o_ref[0:rows, :] = x
        ob_ref[0:rows, :] = (x * gain_ref[...]).astype(ob_ref.dtype)
        rstd = lax.rsqrt(jnp.mean(x * x, axis=-1, keepdims=True) + RMS_EPS)
        rstd_ref[0:rows, :] = jnp.broadcast_to(rstd, (rows, LANES))

    @pl.when(pl.program_id(0) < nblk)
    def _():
        emit(x_ref[...])

    @pl.when(pl.program_id(0) == nblk)
    def _():
        emit(tail_ref[...])


def _stack_rows(x_prompt, tail, gains, gain_idx, *, row_tile=RMS_TILE_OUT):
    p, d = x_prompt.shape
    t = tail.shape[0]
    assert p % row_tile == 0 and t <= row_tile
    nblk = p // row_tile
    rows = lambda width: pl.BlockSpec((row_tile, width), lambda i: (i, 0))
    return pl.pallas_call(
        functools.partial(_stack_kernel, nblk=nblk),
        grid=(nblk + 1,),
        in_specs=[pl.BlockSpec((row_tile, d), lambda i: (jnp.minimum(i, nblk - 1), 0)),
                  pl.BlockSpec((t, d), lambda i: (0, 0)),
                  pl.BlockSpec((None, 1, d), lambda i: (gain_idx, 0, 0))],
        out_specs=[rows(d), rows(d), rows(LANES)],
        out_shape=[jax.ShapeDtypeStruct((p + t, d), F32), jax.ShapeDtypeStruct((p + t, d), BF16),
                   jax.ShapeDtypeStruct((p + t, LANES), F32)],
        compiler_params=_params(("parallel",)),
    )(x_prompt, tail, gains)


def _mm_kernel(*refs, a_widths, nk, tk, has_rstd, has_res, emit_norm):
    it = iter(refs)
    a_refs = [next(it) for _ in a_widths]
    b_ref = next(it)
    rstd_ref = next(it) if has_rstd else None
    res_ref = next(it) if has_res else None
    gain_ref = next(it) if emit_norm else None
    o_ref = next(it)
    ob_ref, rstd_out_ref = (next(it), next(it)) if emit_norm else (None, None)
    acc_ref = next(it) if nk > 1 else None
    ss_ref = next(it) if emit_norm else None
    j, nj = pl.program_id(1), pl.num_programs(1)

    def finish(acc):
        if has_rstd:
            acc = acc * jnp.tile(rstd_ref[...], (1, acc.shape[1] // LANES))
        if has_res:
            acc = acc + res_ref[...]
        o_ref[...] = acc.astype(o_ref.dtype)
        if emit_norm:
            ob_ref[...] = (acc * gain_ref[...]).astype(ob_ref.dtype)
            sq = acc * acc
            part = sq[:, 0:LANES]
            for c in range(1, acc.shape[1] // LANES):
                part = part + sq[:, c * LANES:(c + 1) * LANES]

            total = jnp.where(j == 0, part, ss_ref[...] + part)
            ss_ref[...] = total
            mean_sq = jnp.sum(total, axis=-1, keepdims=True) / (nj * acc.shape[1])
            rstd_out_ref[...] = jnp.broadcast_to(lax.rsqrt(mean_sq + RMS_EPS), rstd_out_ref.shape)

    if nk == 1:
        acc, k0 = None, 0
        for a_ref, kw in zip(a_refs, a_widths):
            part = jnp.dot(a_ref[...], b_ref[k0:k0 + kw, :].astype(BF16), preferred_element_type=F32)
            acc = part if acc is None else acc + part
            k0 += kw
        finish(acc)
        return

    (a_ref,) = a_refs
    k = pl.program_id(2)
    for kk in range(nk):
        @pl.when(k == kk)
        def _(kk=kk):
            part = jnp.dot(a_ref[:, kk * tk:(kk + 1) * tk], b_ref[...].astype(BF16),
                           preferred_element_type=F32)
            if kk == 0:
                acc_ref[...] = part
            elif kk < nk - 1:
                acc_ref[...] += part
            else:
                finish(acc_ref[...] + part)


def _matmul(a_parts, w, layer, *, n, tm, tn=N_TILE, nk=1, rstd=None, res=None, next_gain=None):
    emit_norm = next_gain is not None
    m = a_parts[0].shape[0]
    a_widths = tuple(a.shape[1] for a in a_parts)
    kdim = sum(a_widths)
    assert nk == 1 or len(a_parts) == 1
    assert w.shape[-1] == n or not emit_norm
    tk = kdim // nk
    in_specs = [pl.BlockSpec((tm, kw), lambda i, j, k: (i, 0), pipeline_mode=pl.Buffered(1)) for kw in a_widths]
    in_specs.append(pl.BlockSpec((None, tk, tn), lambda i, j, k: (layer, k, j)))
    args = [*a_parts, w]
    if rstd is not None:
        in_specs.append(pl.BlockSpec((tm, LANES), lambda i, j, k: (i, 0)))
        args.append(rstd)
    if res is not None:
        in_specs.append(pl.BlockSpec((tm, tn), lambda i, j, k: (i, j)))
        args.append(res)
    if emit_norm:
        gains, gain_idx = next_gain
        in_specs.append(pl.BlockSpec((None, 1, tn), lambda i, j, k: (gain_idx, 0, j)))
        args.append(gains)
    tile = pl.BlockSpec((tm, tn), lambda i, j, k: (i, j))
    out_specs, out_shape = [tile], [jax.ShapeDtypeStruct((m, n), F32)]
    scratch = [pltpu.VMEM((tm, tn), F32)] if nk > 1 else []
    if emit_norm:
        out_specs += [tile, pl.BlockSpec((tm, LANES), lambda i, j, k: (i, 0))]
        out_shape += [jax.ShapeDtypeStruct((m, n), BF16), jax.ShapeDtypeStruct((m, LANES), F32)]
        scratch.append(pltpu.VMEM((tm, LANES), F32))
    outs = pl.pallas_call(
        functools.partial(_mm_kernel, a_widths=a_widths, nk=nk, tk=tk, has_rstd=rstd is not None,
                          has_res=res is not None, emit_norm=emit_norm),
        grid=(m // tm, n // tn, nk),
        in_specs=in_specs,
        out_specs=out_specs,
        out_shape=out_shape,
        scratch_shapes=scratch,
        compiler_params=_params(("parallel", "arbitrary", "arbitrary")),
    )(*args)
    return outs if emit_norm else outs[0]


def _mm_gated_kernel(a_ref, bg_ref, bu_ref, rstd_ref, *refs, nj, ride_steps):
    o_ref = refs[-1] if ride_steps == 0 else refs[-2]
    a = a_ref[...]
    rstd = jnp.tile(rstd_ref[...], (1, o_ref.shape[1] // LANES))
    g = jnp.dot(a, bg_ref[...].astype(BF16), preferred_element_type=F32) * rstd
    u = jnp.dot(a, bu_ref[...].astype(BF16), preferred_element_type=F32) * rstd
    o_ref[...] = (g * jax.nn.sigmoid(g) * u).astype(o_ref.dtype)

    if ride_steps:
        old_ref, next_ref, shifted_ref = refs[0], refs[1], refs[-1]
        rows = shifted_ref.shape[0]

        shifted_ref[0:rows - 1] = old_ref[1:rows]
        shifted_ref[rows - 1] = next_ref[0]


def _matmul_gated(a, w, layer, rstd, *, half, tm=ROW_TILE, tn=GATE_TILE, ride=None, ride_rows=128):
    m, kdim = a.shape
    ni, nj = m // tm, half // tn
    in_specs = [
        pl.BlockSpec((tm, kdim), lambda i, j: (i, 0), pipeline_mode=pl.Buffered(1)),
        pl.BlockSpec((None, kdim, tn), lambda i, j: (layer, 0, j)),
        pl.BlockSpec((None, kdim, tn), lambda i, j: (layer, 0, j + nj)),
        pl.BlockSpec((tm, LANES), lambda i, j: (i, 0)),
    ]
    args = [a, w, w, rstd]
    out_specs = [pl.BlockSpec((tm, tn), lambda i, j: (i, j))]
    out_shape = [jax.ShapeDtypeStruct((m, half), BF16)]
    ride_steps = 0
    if ride is not None:
        cache, cache_layer = ride
        _, n, hist, heads, e = cache.shape
        assert hist % ride_rows == 0
        nblk = hist // ride_rows
        ride_steps = n * nblk
        assert ride_steps <= ni * nj

        def seq_blk(i, j):
            step = jnp.minimum(i * nj + j, ride_steps - 1)
            return step // nblk, step % nblk

        def old_map(i, j):
            sq, t = seq_blk(i, j)
            return cache_layer, sq, t, 0, 0

        def next_map(i, j):
            sq, t = seq_blk(i, j)
            return cache_layer, sq, jnp.minimum((t + 1) * ride_rows, hist - 1), 0, 0

        def out_map(i, j):
            sq, t = seq_blk(i, j)
            return sq, t, 0, 0

        in_specs += [pl.BlockSpec((None, None, ride_rows, heads, e), old_map),
                     pl.BlockSpec((None, None, 1, heads, e), next_map)]
        args += [cache, cache]
        out_specs.append(pl.BlockSpec((None, ride_rows, heads, e), out_map))
        out_shape.append(jax.ShapeDtypeStruct(cache.shape[1:], cache.dtype))
    outs = pl.pallas_call(
        functools.partial(_mm_gated_kernel, nj=nj, ride_steps=ride_steps),
        grid=(ni, nj),
        in_specs=in_specs,
        out_specs=out_specs,
        out_shape=out_shape,
        compiler_params=_params(("arbitrary", "arbitrary") if ride_steps else ("parallel", "parallel")),
    )(*args)
    return outs if ride_steps else (outs[0], None)


def _cast_ride(w, layer, rows, step_of, nsteps):
    _, kdim, n = w.shape
    nblk = pl.cdiv(kdim, rows)
    assert nblk <= nsteps
    blk = lambda *idx: jnp.minimum(step_of(*idx), nblk - 1)
    return (pl.BlockSpec((None, rows, n), lambda *idx: (layer, blk(*idx), 0)),
            pl.BlockSpec((rows, n), lambda *idx: (blk(*idx), 0)),
            jax.ShapeDtypeStruct((kdim, n), BF16))


def _gelu(x):
    return 0.5 * x * (1.0 + lax.erf(x * (2.0 ** -0.5)))


def _layer_norm(x, g, b):
    xc = x - jnp.mean(x, axis=-1, keepdims=True)
    y = xc * lax.rsqrt(jnp.mean(xc * xc, axis=-1, keepdims=True) + LN_EPS)
    return y * g + b


def _silu(x):
    return x * jax.nn.sigmoid(x)


def _mixa_prompt_kernel(val_ref, gate_ref, w_ref, cb_ref, g_ref, b_ref, tail_ref, wf_ref, a_ref, st_ref, wb_ref,
                        ext_ref, *, batch, seq, kw, hist_pad, chunk):
    b = pl.program_id(1)
    wb_ref[...] = wf_ref[...].astype(wb_ref.dtype)

    @pl.when(b == batch)
    def _():
        a_ref[0:tail_ref.shape[0], :] = tail_ref[...].astype(a_ref.dtype)

    @pl.when(b < batch)
    def _():
        ext_ref[0:hist_pad, :] = jnp.zeros((hist_pad, LANES), F32)
        ext_ref[hist_pad:hist_pad + seq, :] = val_ref[...] * jax.nn.sigmoid(gate_ref[...])
        st_ref[...] = ext_ref[hist_pad + seq - (kw - 1):hist_pad + seq, :]
        off = hist_pad - (kw - 1)
        for c in range(seq // chunk):
            t0 = c * chunk
            acc = w_ref[0:1, :] * ext_ref[t0 + off:t0 + off + chunk, :]
            for k in range(1, kw):
                acc = acc + w_ref[k:k + 1, :] * ext_ref[t0 + off + k:t0 + off + k + chunk, :]
            y = _layer_norm(acc + cb_ref[...], g_ref[...], b_ref[...])
            a_ref[t0:t0 + chunk, :] = _silu(y).astype(a_ref.dtype)


def _mixa_prompt(z, tail, conv_w, conv_b, norm_g, norm_b, layer, cast, *, batch, seq, width, cast_rows=176):
    kw = conv_w.shape[1]
    nc = width // LANES
    hist_pad = 32
    assert tail.shape[0] == z.shape[0] - batch * seq and tail.shape[0] <= seq
    vec = lambda: pl.BlockSpec((None, 1, LANES), lambda c, b: (layer, 0, c))
    wf_spec, wb_spec, wb_shape = _cast_ride(*cast, cast_rows, lambda c, b: c * (batch + 1) + b, nc * (batch + 1))
    return pl.pallas_call(
        functools.partial(_mixa_prompt_kernel, batch=batch, seq=seq, kw=kw, hist_pad=hist_pad, chunk=128),
        grid=(nc, batch + 1),
        in_specs=[
            pl.BlockSpec((seq, LANES), lambda c, b: (b, c)),
            pl.BlockSpec((seq, LANES), lambda c, b: (b, c + nc)),
            pl.BlockSpec((None, kw, LANES), lambda c, b: (layer, 0, c)),
            vec(), vec(), vec(),
            pl.BlockSpec((tail.shape[0], LANES), lambda c, b: (0, c)),
            wf_spec,
        ],
        out_specs=[
            pl.BlockSpec((seq, LANES), lambda c, b: (b, c)),
            pl.BlockSpec((None, kw - 1, LANES), lambda c, b: (jnp.minimum(b, batch - 1), 0, c)),
            wb_spec,
        ],
        out_shape=[
            jax.ShapeDtypeStruct((z.shape[0], width), BF16),
            jax.ShapeDtypeStruct((batch, kw - 1, width), F32),
            wb_shape,
        ],
        scratch_shapes=[pltpu.VMEM((hist_pad + seq, LANES), F32)],
        compiler_params=_params(("arbitrary", "arbitrary")),
    )(z, z, conv_w, conv_b, norm_g, norm_b, tail, cast[0])


def _mixb_prompt_kernel(u_ref, v_ref, ng_ref, nb_ref, ws_ref, bst_ref, tail_ref, o_ref, *, groups, nchunks):
    step = pl.program_id(0)

    @pl.when(step == nchunks)
    def _():
        o_ref[...] = tail_ref[...].astype(o_ref.dtype)

    @pl.when(step < nchunks)
    def _():
        v = _layer_norm(_gelu(v_ref[...]), ng_ref[...], nb_ref[...]).astype(BF16)
        n = ws_ref.shape[-1]
        causal = lax.broadcasted_iota(jnp.int32, (n, n), 1) <= lax.broadcasted_iota(jnp.int32, (n, n), 0)
        for g in range(groups):
            cols = slice(g * LANES, (g + 1) * LANES)
            w_m = jnp.where(causal, ws_ref[g], 0.0).astype(BF16)
            s = jnp.dot(w_m, v[:, cols], preferred_element_type=F32) + bst_ref[:, g:g + 1]
            o_ref[:, cols] = (_gelu(u_ref[:, cols]) * s).astype(o_ref.dtype)


def _mixb_prompt(z, tail, norm_g, norm_b, spatial_w, spatial_bt, layer, *, rows, width):
    groups, chunk = spatial_w.shape[1], spatial_w.shape[2]
    assert tail.shape[0] == z.shape[0] - rows == chunk
    vec = lambda: pl.BlockSpec((None, 1, width), lambda s: (layer, 0, 0))
    return pl.pallas_call(
        functools.partial(_mixb_prompt_kernel, groups=groups, nchunks=rows // chunk),
        grid=(rows // chunk + 1,),
        in_specs=[
            pl.BlockSpec((chunk, width), lambda s: (s, 2)),
            pl.BlockSpec((chunk, width), lambda s: (s, 3)),
            vec(), vec(),
            pl.BlockSpec((None, groups, chunk, chunk), lambda s: (layer, 0, 0, 0)),
            pl.BlockSpec((None, chunk, groups), lambda s: (layer, 0, 0)),
            pl.BlockSpec((chunk, width), lambda s: (0, 1)),
        ],
        out_specs=pl.BlockSpec((chunk, width), lambda s: (s, 0)),
        out_shape=jax.ShapeDtypeStruct((z.shape[0], width), BF16),
        compiler_params=_params(("parallel",)),
    )(z, z, norm_g, norm_b, spatial_w, spatial_bt, tail)


def _mix_sample_kernel(val_ref, gate_ref, u_ref, v_ref, st_ref, cw_ref, cb_ref, ag_ref, ab_ref,
                       bg_ref, bb_ref, sw_ref, sb_ref, y_ref, ain_ref, vout_ref, *, kw, groups, width):
    n = val_ref.shape[0]
    y_ref[...] = jnp.zeros(y_ref.shape, y_ref.dtype)
    a_in = val_ref[...] * jax.nn.sigmoid(gate_ref[...])
    ain_ref[...] = a_in
    acc = cw_ref[0:1, :] * st_ref[0]
    for k in range(1, kw - 1):
        acc = acc + cw_ref[k:k + 1, :] * st_ref[k]
    acc = acc + cw_ref[kw - 1:kw, :] * a_in + cb_ref[...]
    for g in range(groups):
        cols = slice(g * LANES, (g + 1) * LANES)
        y = _layer_norm(acc[:, cols], ag_ref[:, cols], ab_ref[:, cols])
        y_ref[0:n, cols] = _silu(y)
    v = _layer_norm(_gelu(v_ref[...]), bg_ref[...], bb_ref[...])
    vout_ref[...] = v
    y_ref[0:n, width:2 * width] = _gelu(u_ref[...]) * (sw_ref[...] * v + sb_ref[...])


def _mix_sample(z, state_t, conv_w, conv_b, a_g, a_b, b_g, b_b, s_w0, s_b0, layer, *, row0, n, width):
    tail = z.shape[0] - row0
    kw = conv_w.shape[1]
    rb = row0 // n
    zspec = lambda c: pl.BlockSpec((n, width), lambda i: (rb, c))
    vec = lambda: pl.BlockSpec((None, 1, width), lambda i: (layer, 0, 0))
    return pl.pallas_call(
        functools.partial(_mix_sample_kernel, kw=kw, groups=A_GROUPS, width=width),
        grid=(1,),
        in_specs=[
            zspec(0), zspec(1), zspec(2), zspec(3),
            pl.BlockSpec((kw - 1, n, width), lambda i: (0, 0, 0)),
            pl.BlockSpec((None, kw, width), lambda i: (layer, 0, 0)),
            vec(), vec(), vec(), vec(), vec(), vec(), vec(),
        ],
        out_specs=[
            pl.BlockSpec((tail, 2 * width), lambda i: (0, 0)),
            pl.BlockSpec((n, width), lambda i: (0, 0)),
            pl.BlockSpec((n, width), lambda i: (0, 0)),
        ],
        out_shape=[
            jax.ShapeDtypeStruct((tail, 2 * width), F32),
            jax.ShapeDtypeStruct((n, width), F32),
            jax.ShapeDtypeStruct((n, width), F32),
        ],
        compiler_params=_params(("arbitrary",)),
    )(z, z, z, z, state_t, conv_w, conv_b, a_g, a_b, b_g, b_b, s_w0, s_b0)


def _rows(start, size, stride):
    return pl.ds(start, size) if stride == 1 else pl.ds(start, size, stride=stride)


LOG2E = 1.4426950408889634


def _attn_prompt_kernel(q_ref, k_ref, v_ref, tail_ref, wf_ref, o_ref, wb_ref, bias1_ref, bias2_ref, *stats,
                        batch, patterns, blk, scale):
    wb_ref[...] = wf_ref[...].astype(wb_ref.dtype)

    @pl.when(pl.program_id(0) == batch)
    def _():
        o_ref[0:tail_ref.shape[0], :] = tail_ref[...].astype(o_ref.dtype)

    @pl.when(pl.program_id(0) < batch)
    def _():
        _attn_prompt_sequence(q_ref, k_ref, v_ref, o_ref, bias1_ref, bias2_ref, stats, patterns, blk, scale)


def _attn_prompt_sequence(q_ref, k_ref, v_ref, o_ref, bias1_ref, bias2_ref, stats, patterns, blk, scale):
    seq, e = q_ref.shape
    npat = len(patterns)
    acc_refs, m_refs, l_refs = stats[:npat], stats[npat:2 * npat], stats[2 * npat:]
    c = scale * LOG2E

    def distance(nk):
        qi = lax.broadcasted_iota(jnp.int32, (blk, nk), 0)
        kj = lax.broadcasted_iota(jnp.int32, (blk, nk), 1)
        return (nk - blk) + qi - kj

    def attend(q_rows, k_rows, bias):
        qb = q_ref[q_rows, :].astype(BF16)
        kb = k_ref[k_rows, :].astype(BF16)
        vb = v_ref[k_rows, :].astype(BF16)
        s = lax.dot_general(qb, kb, (((1,), (1,)), ((), ())), preferred_element_type=F32) + bias
        m = jnp.max(s, axis=-1, keepdims=True)
        p = jnp.exp2((s - m) * c).astype(BF16)
        ov = jnp.dot(p, jnp.concatenate([vb, jnp.ones_like(vb)], axis=1), preferred_element_type=F32)
        return ov[:, :e], ov[:, e:], jnp.broadcast_to(m, (blk, e))

    for p, (window, dil) in enumerate(patterns):
        reach = window // dil
        nblk = seq // dil // blk
        for bias_ref in (bias1_ref, bias2_ref):
            dist = distance(bias_ref.shape[1])
            bias_ref[...] = jnp.where((dist >= 0) & (dist <= reach), 0.0, -jnp.inf)
        for r in range(dil):
            for b in range(nblk):
                q_rows = _rows(r + dil * blk * b, blk, dil)
                if b == 0:
                    k_rows, bias = q_rows, bias1_ref[...]
                else:
                    k_rows, bias = _rows(r + dil * blk * (b - 1), 2 * blk, dil), bias2_ref[...]
                acc_refs[p][q_rows, :], l_refs[p][q_rows, :], m_refs[p][q_rows, :] = attend(q_rows, k_rows, bias)

    def merge(t, carry):
        rows = pl.ds(pl.multiple_of(t * blk, blk), blk)
        ms = [m_ref[rows, :] for m_ref in m_refs]
        top = functools.reduce(jnp.maximum, ms)
        ws = [jnp.exp2((m - top) * c) for m in ms]
        den = functools.reduce(lambda x, y: x + y, [w * l_ref[rows, :] for w, l_ref in zip(ws, l_refs)])
        num = functools.reduce(lambda x, y: x + y, [w * a_ref[rows, :] for w, a_ref in zip(ws, acc_refs)])
        o_ref[rows, :] = (num / den).astype(o_ref.dtype)
        return carry

    lax.fori_loop(0, seq // blk, merge, 0, unroll=2)


def _attn_prompt(qkv, tail, cast, *, batch, seq, heads, cast_rows=128):
    e = qkv.shape[1] // (3 * heads)
    assert e == LANES and tail.shape[0] == qkv.shape[0] - batch * seq and tail.shape[0] <= seq
    for window, dil in C_PATTERNS:
        assert window // dil <= C_BLOCK and (seq // dil) % C_BLOCK == 0
    spec = pl.BlockSpec((seq, e), lambda b, h: (b, h))
    part = lambda c: pl.BlockSpec((seq, e), lambda b, h: (b, c * heads + h))
    nstats = 3 * len(C_PATTERNS)
    wf_spec, wb_spec, wb_shape = _cast_ride(*cast, cast_rows, lambda b, h: b * heads + h, (batch + 1) * heads)
    return pl.pallas_call(
        functools.partial(_attn_prompt_kernel, batch=batch, patterns=C_PATTERNS, blk=C_BLOCK, scale=e ** -0.5),
        grid=(batch + 1, heads),
        in_specs=[part(0), part(1), part(2), pl.BlockSpec((tail.shape[0], e), lambda b, h: (0, h)), wf_spec],
        out_specs=[spec, wb_spec],
        out_shape=[jax.ShapeDtypeStruct((qkv.shape[0], heads * e), BF16), wb_shape],
        scratch_shapes=(
            [pltpu.VMEM((C_BLOCK, C_BLOCK), F32), pltpu.VMEM((C_BLOCK, 2 * C_BLOCK), F32)]
            + [pltpu.VMEM((seq, e), F32) for _ in range(nstats)]
        ),
        compiler_params=_params(("arbitrary", "arbitrary")),
    )(qkv, qkv, qkv, tail, cast[0])


def _attn_sample_kernel(q_ref, kn_ref, vn_ref, *refs, npat, scale):
    kc_refs, vc_refs, o_ref = refs[:npat], refs[npat:2 * npat], refs[2 * npat]
    q = q_ref[...]
    s_new = jnp.sum(q * kn_ref[...], axis=-1, keepdims=True) * scale
    scores = [jnp.sum(kc[...] * q[None], axis=-1, keepdims=True) * scale for kc in kc_refs]
    top = s_new
    for s in scores:
        top = jnp.maximum(top, jnp.max(s, axis=0))
    w_new = npat * jnp.exp(s_new - top)
    den = w_new
    num = w_new * vn_ref[...]
    for s, vc in zip(scores, vc_refs):
        p = jnp.exp(s - top[None])
        den = den + jnp.sum(p, axis=0)
        num = num + jnp.sum(p * vc[...], axis=0)
    o_ref[...] = num / den


def _attn_sample(q_s, k_s, v_s, cache_k, cache_v, layer, *, head_group=32):
    n, heads, e = q_s.shape
    hist = cache_k.shape[2]
    new = pl.BlockSpec((None, head_group, e), lambda i, g: (i, g, 0))
    views, specs = [], []
    for window, dil in C_PATTERNS:
        reach = window // dil
        assert window <= hist and hist % dil == 0 and (hist // dil) % reach == 0
        last = hist // dil // reach - 1
        views.append((cache_k.shape[0], n, hist // dil, dil, heads, e))
        specs.append(pl.BlockSpec((None, None, reach, None, head_group, e),
                                  lambda i, g, last=last: (layer, i, last, 0, g, 0)))
    return pl.pallas_call(
        functools.partial(_attn_sample_kernel, npat=len(C_PATTERNS), scale=e ** -0.5),
        grid=(n, heads // head_group),
        in_specs=[new, new, new] + specs + specs,
        out_specs=new,
        out_shape=jax.ShapeDtypeStruct((n, heads, e), F32),
        compiler_params=_params(("parallel", "parallel")),
    )(q_s, k_s, v_s, *[cache_k.reshape(v) for v in views], *[cache_v.reshape(v) for v in views])


def _cache_set_last_kernel(new_ref, shifted_ref, o_ref):
    del shifted_ref
    o_ref[...] = new_ref[...]


def _cache_set_last(shifted, new):
    n, hist, heads, e = shifted.shape
    return pl.pallas_call(
        _cache_set_last_kernel,
        grid=(n,),
        in_specs=[pl.BlockSpec((None, 1, heads, e), lambda i: (i, 0, 0, 0)), pl.BlockSpec(memory_space=pl.ANY)],
        out_specs=pl.BlockSpec((None, 1, heads, e), lambda i: (i, hist - 1, 0, 0)),
        out_shape=jax.ShapeDtypeStruct(shifted.shape, shifted.dtype),
        input_output_aliases={1: 0},
        compiler_params=_params(("parallel",)),
    )(new, shifted)


def kernel(x_prompt, x_sample, state_a_conv, cache_c_k, cache_c_v, ln_mix_even, w_in_even, a_conv_w,
           a_conv_b, a_norm_g, a_norm_b, b_norm_g, b_norm_b, b_spatial_w, b_spatial_b, w_out_even,
           ln_mix_odd, w_qkv_odd, w_o_odd, ln_ffn, w_gate_up, w_down, ln_final):
    batch, seq, d = x_prompt.shape
    nsamp = x_sample.shape[0] * x_sample.shape[1]
    depth = ln_ffn.shape[0]
    np_rows = batch * seq
    rows_all = pl.cdiv(np_rows + nsamp, ROW_TILE) * ROW_TILE
    tail = rows_all - np_rows
    assert rows_all % NORM_ROW_TILE == 0 and np_rows % RMS_TILE_OUT == 0 and np_rows % nsamp == 0
    width = a_conv_w.shape[-1]
    ffn = w_down.shape[1]
    e = d // C_HEADS

    def tail_rows(s):
        return jnp.concatenate([s, jnp.zeros((tail - nsamp, s.shape[1]), s.dtype)], axis=0)

    vec3 = lambda t: t.reshape(t.shape[0], 1, t.shape[-1])
    ln_mix = (vec3(ln_mix_even), vec3(ln_mix_odd))

    def mix_gain(i):
        return ln_mix[i % 2], i // 2

    x, xb, rstd = _stack_rows(x_prompt.reshape(np_rows, d), tail_rows(x_sample.reshape(nsamp, d)), *mix_gain(0))

    a_hist_p, a_hist_s, b_v_s = [], [], []
    c_k_p, c_v_p = [], []
    rides = [(c, j) for j in range(cache_c_k.shape[0]) for c in (cache_c_k, cache_c_v)]
    assert len(rides) <= depth
    shifted, new_rows = [], []
    for i in range(depth):
        j = i // 2
        if i % 2 == 0:
            z = _matmul([xb], w_in_even, j, n=w_in_even.shape[-1], tm=ROW_TILE, rstd=rstd)
            s_w0 = jnp.repeat(b_spatial_w[:, :, 0, 0], width // B_GROUPS, axis=-1)
            s_b0 = jnp.repeat(b_spatial_b[:, :, 0], width // B_GROUPS, axis=-1)
            y_s, a_in_s, v_s = _mix_sample(
                z, jnp.swapaxes(state_a_conv[j], 0, 1), a_conv_w, vec3(a_conv_b), vec3(a_norm_g),
                vec3(a_norm_b), vec3(b_norm_g), vec3(b_norm_b), vec3(s_w0), vec3(s_b0), j,
                row0=np_rows, n=nsamp, width=width)
            a, hist_p, w_down_bf16 = _mixa_prompt(z, y_s, a_conv_w, vec3(a_conv_b), vec3(a_norm_g), vec3(a_norm_b),
                                                  j, (w_down, i), batch=batch, seq=seq, width=width)
            b = _mixb_prompt(z, y_s, vec3(b_norm_g), vec3(b_norm_b), b_spatial_w,
                             jnp.swapaxes(b_spatial_b, 1, 2), j, rows=np_rows, width=width)
            x, xb, rstd = _matmul([a, b], w_out_even, j, n=d, tm=NORM_ROW_TILE, res=x, next_gain=(vec3(ln_ffn), i))
            a_hist_p.append(hist_p)
            a_hist_s.append(jnp.concatenate([state_a_conv[j][:, 1:], a_in_s[:, None, :]], axis=1))
            b_v_s.append(v_s.reshape(x_sample.shape[0], x_sample.shape[1], width))
        else:
            qkv = _matmul([xb], w_qkv_odd, j, n=3 * d, tm=ROW_TILE, rstd=rstd)
            qkv_s = qkv[np_rows:np_rows + nsamp].reshape(nsamp, 3, C_HEADS, e)
            o_s = _attn_sample(qkv_s[:, 0], qkv_s[:, 1], qkv_s[:, 2], cache_c_k, cache_c_v, j)
            o, w_down_bf16 = _attn_prompt(qkv, tail_rows(o_s.reshape(nsamp, d)), (w_down, i),
                                          batch=batch, seq=seq, heads=C_HEADS)
            x, xb, rstd = _matmul([o], w_o_odd, j, n=d, tm=NORM_ROW_TILE, res=x, next_gain=(vec3(ln_ffn), i))
            keep = min(C_PATTERNS[-1][0], seq)
            c_k_p.append(qkv[:np_rows, d:2 * d].reshape(batch, seq, C_HEADS, e)[:, seq - keep:])
            c_v_p.append(qkv[:np_rows, 2 * d:].reshape(batch, seq, C_HEADS, e)[:, seq - keep:])
            new_rows += [qkv_s[:, 1:2], qkv_s[:, 2:3]]
        hid, bulk = _matmul_gated(xb, w_gate_up, i, rstd, half=ffn, ride=rides[i] if i < len(rides) else None)
        shifted.append(bulk)
        down = functools.partial(_matmul, [hid], w_down_bf16[None], 0, n=d, tm=DOWN_ROW_TILE, nk=2, res=x)
        if i + 1 < depth:
            x, xb, rstd = down(next_gain=mix_gain(i + 1))
        else:
            x = down()

    c_s = [_cache_set_last(bulk, new) for bulk, new in zip(shifted, new_rows)]
    c_k_s, c_v_s = c_s[0::2], c_s[1::2]
    g_final = ln_final.reshape(1, 1, d)
    y_prompt = _rmsnorm(x, g_final, 0, F32, rows=np_rows, row_tile=RMS_TILE_OUT)
    y_sample = _rmsnorm(x, g_final, 0, F32, rows=nsamp, row_tile=nsamp, first_block=np_rows // nsamp)
    return (y_prompt.reshape(batch, seq, d), y_sample.reshape(x_sample.shape),
            jnp.stack(a_hist_p), jnp.stack(a_hist_s), jnp.stack(b_v_s),
            jnp.stack(c_k_p), jnp.stack(c_v_p), jnp.stack(c_k_s), jnp.stack(c_v_s))
```

```python
import functools

import jax
import jax.numpy as jnp
from jax import lax
from jax.experimental import pallas as pl
from jax.experimental.pallas import tpu as pltpu

F32 = jnp.float32
BF16 = jnp.bfloat16

LANES = 128
VMEM_BYTES = 64 * 1024 * 1024
VMEM_LIMIT = VMEM_BYTES - 8 * 1024 * 1024

RMS_EPS = 1e-6
LN_EPS = 1e-5

A_GROUPS = 16
B_GROUPS = 16
B_CHUNK = 128
C_HEADS = 32
C_PATTERNS = ((128, 1), (512, 4), (2048, 16))
C_BLOCK = 128

ROW_TILE = 2080
DOWN_ROW_TILE = 1040
N_TILE = 512
GATE_TILE = 256
NORM_ROW_TILE = 1664
RMS_TILE_OUT = 256


def _params(semantics):
    return pltpu.CompilerParams(dimension_semantics=semantics, vmem_limit_bytes=VMEM_LIMIT)


def _rms_kernel(x_ref, g_ref, o_ref):
    x = x_ref[...]
    y = x * lax.rsqrt(jnp.mean(x * x, axis=-1, keepdims=True) + RMS_EPS)
    o_ref[...] = (y * g_ref[...]).astype(o_ref.dtype)


def _rmsnorm(x, gains, layer, out_dtype, *, rows, row_tile, first_block=0):
    d = x.shape[1]
    return pl.pallas_call(
        _rms_kernel,
        grid=(rows // row_tile,),
        in_specs=[
            pl.BlockSpec((row_tile, d), lambda i: (i + first_block, 0)),
            pl.BlockSpec((None, 1, d), lambda i: (layer, 0, 0)),
        ],
        out_specs=pl.BlockSpec((row_tile, d), lambda i: (i, 0)),
        out_shape=jax.ShapeDtypeStruct((rows, d), out_dtype),
        compiler_params=_params(("parallel",)),
    )(x, gains)


def _stack_kernel(x_ref, tail_ref, gain_ref, o_ref, ob_ref, rstd_ref, *, nblk):
    def emit(x):
        rows = x.shape[0]
        o_ref[0:rows, :] = x
        ob_ref[0:rows, :] = (x * gain_ref[...]).astype(ob_ref.dtype)
        rstd = lax.rsqrt(jnp.mean(x * x, axis=-1, keepdims=True) + RMS_EPS)
        rstd_ref[0:rows, :] = jnp.broadcast_to(rstd, (rows, LANES))

    @pl.when(pl.program_id(0) < nblk)
    def _():
        emit(x_ref[...])

    @pl.when(pl.program_id(0) == nblk)
    def _():
        emit(tail_ref[...])


def _stack_rows(x_prompt, tail, gains, gain_idx, *, row_tile=RMS_TILE_OUT):
    p, d = x_prompt.shape
    t = tail.shape[0]
    assert p % row_tile == 0 and t <= row_tile
    nblk = p // row_tile
    rows = lambda width: pl.BlockSpec((row_tile, width), lambda i: (i, 0))
    return pl.pallas_call(
        functools.partial(_stack_kernel, nblk=nblk),
        grid=(nblk + 1,),
        in_specs=[pl.BlockSpec((row_tile, d), lambda i: (jnp.minimum(i, nblk - 1), 0)),
                  pl.BlockSpec((t, d), lambda i: (0, 0)),
                  pl.BlockSpec((None, 1, d), lambda i: (gain_idx, 0, 0))],
        out_specs=[rows(d), rows(d), rows(LANES)],
        out_shape=[jax.ShapeDtypeStruct((p + t, d), F32), jax.ShapeDtypeStruct((p + t, d), BF16),
                   jax.ShapeDtypeStruct((p + t, LANES), F32)],
        compiler_params=_params(("parallel",)),
    )(x_prompt, tail, gains)


def _mm_kernel(*refs, a_widths, nk, tk, has_rstd, has_res, emit_norm):
    it = iter(refs)
    a_refs = [next(it) for _ in a_widths]
    b_ref = next(it)
    rstd_ref = next(it) if has_rstd else None
    res_ref = next(it) if has_res else None
    gain_ref = next(it) if emit_norm else None
    o_ref = next(it)
    ob_ref, rstd_out_ref = (next(it), next(it)) if emit_norm else (None, None)
    acc_ref = next(it) if nk > 1 else None
    ss_ref = next(it) if emit_norm else None
    j, nj = pl.program_id(1), pl.num_programs(1)

    def finish(acc):
        if has_rstd:
            acc = acc * jnp.tile(rstd_ref[...], (1, acc.shape[1] // LANES))
        if has_res:
            acc = acc + res_ref[...]
        o_ref[...] = acc.astype(o_ref.dtype)
        if emit_norm:
            ob_ref[...] = (acc * gain_ref[...]).astype(ob_ref.dtype)
            sq = acc * acc
            part = sq[:, 0:LANES]
            for c in range(1, acc.shape[1] // LANES):
                part = part + sq[:, c * LANES:(c + 1) * LANES]

            total = jnp.where(j == 0, part, ss_ref[...] + part)
            ss_ref[...] = total
            mean_sq = jnp.sum(total, axis=-1, keepdims=True) / (nj * acc.shape[1])
            rstd_out_ref[...] = jnp.broadcast_to(lax.rsqrt(mean_sq + RMS_EPS), rstd_out_ref.shape)

    if nk == 1:
        acc, k0 = None, 0
        for a_ref, kw in zip(a_refs, a_widths):
            part = jnp.dot(a_ref[...], b_ref[k0:k0 + kw, :].astype(BF16), preferred_element_type=F32)
            acc = part if acc is None else acc + part
            k0 += kw
        finish(acc)
        return

    (a_ref,) = a_refs
    k = pl.program_id(2)
    for kk in range(nk):
        @pl.when(k == kk)
        def _(kk=kk):
            part = jnp.dot(a_ref[:, kk * tk:(kk + 1) * tk], b_ref[...].astype(BF16),
                           preferred_element_type=F32)
            if kk == 0:
                acc_ref[...] = part
            elif kk < nk - 1:
                acc_ref[...] += part
            else:
                finish(acc_ref[...] + part)


def _matmul(a_parts, w, layer, *, n, tm, tn=N_TILE, nk=1, rstd=None, res=None, next_gain=None):
    emit_norm = next_gain is not None
    m = a_parts[0].shape[0]
    a_widths = tuple(a.shape[1] for a in a_parts)
    kdim = sum(a_widths)
    assert nk == 1 or len(a_parts) == 1
    assert w.shape[-1] == n or not emit_norm
    tk = kdim // nk
    in_specs = [pl.BlockSpec((tm, kw), lambda i, j, k: (i, 0), pipeline_mode=pl.Buffered(1)) for kw in a_widths]
    in_specs.append(pl.BlockSpec((None, tk, tn), lambda i, j, k: (layer, k, j)))
    args = [*a_parts, w]
    if rstd is not None:
        in_specs.append(pl.BlockSpec((tm, LANES), lambda i, j, k: (i, 0)))
        args.append(rstd)
    if res is not None:
        in_specs.append(pl.BlockSpec((tm, tn), lambda i, j, k: (i, j)))
        args.append(res)
    if emit_norm:
        gains, gain_idx = next_gain
        in_specs.append(pl.BlockSpec((None, 1, tn), lambda i, j, k: (gain_idx, 0, j)))
        args.append(gains)
    tile = pl.BlockSpec((tm, tn), lambda i, j, k: (i, j))
    out_specs, out_shape = [tile], [jax.ShapeDtypeStruct((m, n), F32)]
    scratch = [pltpu.VMEM((tm, tn), F32)] if nk > 1 else []
    if emit_norm:
        out_specs += [tile, pl.BlockSpec((tm, LANES), lambda i, j, k: (i, 0))]
        out_shape += [jax.ShapeDtypeStruct((m, n), BF16), jax.ShapeDtypeStruct((m, LANES), F32)]
        scratch.append(pltpu.VMEM((tm, LANES), F32))
    outs = pl.pallas_call(
        functools.partial(_mm_kernel, a_widths=a_widths, nk=nk, tk=tk, has_rstd=rstd is not None,
                          has_res=res is not None, emit_norm=emit_norm),
        grid=(m // tm, n // tn, nk),
        in_specs=in_specs,
        out_specs=out_specs,
        out_shape=out_shape,
        scratch_shapes=scratch,
        compiler_params=_params(("parallel", "arbitrary", "arbitrary")),
    )(*args)
    return outs if emit_norm else outs[0]


def _cast_ride(w, layer, rows, step_of, nsteps):
    _, kdim, n = w.shape
    nblk = pl.cdiv(kdim, rows)
    assert nblk <= nsteps
    blk = lambda *idx: jnp.minimum(step_of(*idx), nblk - 1)
    return (pl.BlockSpec((None, rows, n), lambda *idx: (layer, blk(*idx), 0)),
            pl.BlockSpec((rows, n), lambda *idx: (blk(*idx), 0)),
            jax.ShapeDtypeStruct((kdim, n), BF16))


def _mm_gated_kernel(*refs, has_ride):
    it = iter(refs)
    a_ref, bg_ref, bu_ref, rstd_ref, wf_ref = (next(it) for _ in range(5))
    old_ref, next_ref = (next(it), next(it)) if has_ride else (None, None)
    o_ref, wb_ref = next(it), next(it)
    shifted_ref = next(it) if has_ride else None
    a = a_ref[...]
    rstd = jnp.tile(rstd_ref[...], (1, o_ref.shape[1] // LANES))
    g = jnp.dot(a, bg_ref[...].astype(BF16), preferred_element_type=F32) * rstd
    u = jnp.dot(a, bu_ref[...].astype(BF16), preferred_element_type=F32) * rstd
    o_ref[...] = (g * jax.nn.sigmoid(g) * u).astype(o_ref.dtype)
    wb_ref[...] = wf_ref[...].astype(wb_ref.dtype)
    if has_ride:
        rows = shifted_ref.shape[0]
        shifted_ref[0:rows - 1] = old_ref[1:rows]
        shifted_ref[rows - 1] = next_ref[0]


def _matmul_gated(a, w, layer, rstd, cast, *, half, tm=ROW_TILE, tn=GATE_TILE, ride=None, ride_rows=128,
                  cast_rows=64):
    m, kdim = a.shape
    ni, nj = m // tm, half // tn
    wf_spec, wb_spec, wb_shape = _cast_ride(*cast, cast_rows, lambda i, j: i * nj + j, ni * nj)
    in_specs = [
        pl.BlockSpec((tm, kdim), lambda i, j: (i, 0), pipeline_mode=pl.Buffered(1)),
        pl.BlockSpec((None, kdim, tn), lambda i, j: (layer, 0, j)),
        pl.BlockSpec((None, kdim, tn), lambda i, j: (layer, 0, j + nj)),
        pl.BlockSpec((tm, LANES), lambda i, j: (i, 0)),
        wf_spec,
    ]
    args = [a, w, w, rstd, cast[0]]
    out_specs = [pl.BlockSpec((tm, tn), lambda i, j: (i, j)), wb_spec]
    out_shape = [jax.ShapeDtypeStruct((m, half), BF16), wb_shape]
    if ride is not None:
        cache, cache_layer = ride
        _, n, hist, heads, e = cache.shape
        assert hist % ride_rows == 0
        nblk = hist // ride_rows
        ride_steps = n * nblk
        assert ride_steps <= ni * nj

        def seq_blk(i, j):
            step = jnp.minimum(i * nj + j, ride_steps - 1)
            return step // nblk, step % nblk

        def old_map(i, j):
            sq, t = seq_blk(i, j)
            return cache_layer, sq, t, 0, 0

        def next_map(i, j):
            sq, t = seq_blk(i, j)
            return cache_layer, sq, jnp.minimum((t + 1) * ride_rows, hist - 1), 0, 0

        def out_map(i, j):
            sq, t = seq_blk(i, j)
            return sq, t, 0, 0

        in_specs += [pl.BlockSpec((None, None, ride_rows, heads, e), old_map),
                     pl.BlockSpec((None, None, 1, heads, e), next_map)]
        args += [cache, cache]
        out_specs.append(pl.BlockSpec((None, ride_rows, heads, e), out_map))
        out_shape.append(jax.ShapeDtypeStruct(cache.shape[1:], cache.dtype))
    outs = pl.pallas_call(
        functools.partial(_mm_gated_kernel, has_ride=ride is not None),
        grid=(ni, nj),
        in_specs=in_specs,
        out_specs=out_specs,
        out_shape=out_shape,
        compiler_params=_params(("arbitrary", "arbitrary")),
    )(*args)
    return tuple(outs) if ride is not None else (outs[0], outs[1], None)


def _gelu(x):
    return 0.5 * x * (1.0 + lax.erf(x * (2.0 ** -0.5)))


def _layer_norm(x, g, b):
    xc = x - jnp.mean(x, axis=-1, keepdims=True)
    y = xc * lax.rsqrt(jnp.mean(xc * xc, axis=-1, keepdims=True) + LN_EPS)
    return y * g + b


def _silu(x):
    return x * jax.nn.sigmoid(x)


def _mixa_prompt_kernel(val_ref, gate_ref, w_ref, cb_ref, g_ref, b_ref, tail_ref, a_ref, st_ref, ext_ref,
                        *, batch, seq, kw, hist_pad, chunk):
    b = pl.program_id(1)

    @pl.when(b == batch)
    def _():
        a_ref[0:tail_ref.shape[0], :] = tail_ref[...].astype(a_ref.dtype)

    @pl.when(b < batch)
    def _():
        ext_ref[0:hist_pad, :] = jnp.zeros((hist_pad, LANES), F32)
        ext_ref[hist_pad:hist_pad + seq, :] = val_ref[...] * jax.nn.sigmoid(gate_ref[...])
        st_ref[...] = ext_ref[hist_pad + seq - (kw - 1):hist_pad + seq, :]
        off = hist_pad - (kw - 1)
        for c in range(seq // chunk):
            t0 = c * chunk
            acc = w_ref[0:1, :] * ext_ref[t0 + off:t0 + off + chunk, :]
            for k in range(1, kw):
                acc = acc + w_ref[k:k + 1, :] * ext_ref[t0 + off + k:t0 + off + k + chunk, :]
            y = _layer_norm(acc + cb_ref[...], g_ref[...], b_ref[...])
            a_ref[t0:t0 + chunk, :] = _silu(y).astype(a_ref.dtype)


def _mixa_prompt(z, tail, conv_w, conv_b, norm_g, norm_b, layer, *, batch, seq, width):
    kw = conv_w.shape[1]
    nc = width // LANES
    hist_pad = 32
    assert tail.shape[0] == z.shape[0] - batch * seq and tail.shape[0] <= seq
    vec = lambda: pl.BlockSpec((None, 1, LANES), lambda c, b: (layer, 0, c))
    return pl.pallas_call(
        functools.partial(_mixa_prompt_kernel, batch=batch, seq=seq, kw=kw, hist_pad=hist_pad, chunk=128),
        grid=(nc, batch + 1),
        in_specs=[
            pl.BlockSpec((seq, LANES), lambda c, b: (b, c)),
            pl.BlockSpec((seq, LANES), lambda c, b: (b, c + nc)),
            pl.BlockSpec((None, kw, LANES), lambda c, b: (layer, 0, c)),
            vec(), vec(), vec(),
            pl.BlockSpec((tail.shape[0], LANES), lambda c, b: (0, c)),
        ],
        out_specs=[
            pl.BlockSpec((seq, LANES), lambda c, b: (b, c)),
            pl.BlockSpec((None, kw - 1, LANES), lambda c, b: (jnp.minimum(b, batch - 1), 0, c)),
        ],
        out_shape=[
            jax.ShapeDtypeStruct((z.shape[0], width), BF16),
            jax.ShapeDtypeStruct((batch, kw - 1, width), F32),
        ],
        scratch_shapes=[pltpu.VMEM((hist_pad + seq, LANES), F32)],
        compiler_params=_params(("parallel", "arbitrary")),
    )(z, z, conv_w, conv_b, norm_g, norm_b, tail)


def _mixb_prompt_kernel(u_ref, v_ref, ng_ref, nb_ref, ws_ref, bst_ref, tail_ref, o_ref, *, groups, nchunks):
    step = pl.program_id(0)

    @pl.when(step == nchunks)
    def _():
        o_ref[...] = tail_ref[...].astype(o_ref.dtype)

    @pl.when(step < nchunks)
    def _():
        v = _layer_norm(_gelu(v_ref[...]), ng_ref[...], nb_ref[...]).astype(BF16)
        n = ws_ref.shape[-1]
        causal = lax.broadcasted_iota(jnp.int32, (n, n), 1) <= lax.broadcasted_iota(jnp.int32, (n, n), 0)
        for g in range(groups):
            cols = slice(g * LANES, (g + 1) * LANES)
            w_m = jnp.where(causal, ws_ref[g], 0.0).astype(BF16)
            s = jnp.dot(w_m, v[:, cols], preferred_element_type=F32) + bst_ref[:, g:g + 1]
            o_ref[:, cols] = (_gelu(u_ref[:, cols]) * s).astype(o_ref.dtype)


def _mixb_prompt(z, tail, norm_g, norm_b, spatial_w, spatial_bt, layer, *, rows, width):
    groups, chunk = spatial_w.shape[1], spatial_w.shape[2]
    assert tail.shape[0] == z.shape[0] - rows == chunk
    vec = lambda: pl.BlockSpec((None, 1, width), lambda s: (layer, 0, 0))
    return pl.pallas_call(
        functools.partial(_mixb_prompt_kernel, groups=groups, nchunks=rows // chunk),
        grid=(rows // chunk + 1,),
        in_specs=[
            pl.BlockSpec((chunk, width), lambda s: (s, 2)),
            pl.BlockSpec((chunk, width), lambda s: (s, 3)),
            vec(), vec(),
            pl.BlockSpec((None, groups, chunk, chunk), lambda s: (layer, 0, 0, 0)),
            pl.BlockSpec((None, chunk, groups), lambda s: (layer, 0, 0)),
            pl.BlockSpec((chunk, width), lambda s: (0, 1)),
        ],
        out_specs=pl.BlockSpec((chunk, width), lambda s: (s, 0)),
        out_shape=jax.ShapeDtypeStruct((z.shape[0], width), BF16),
        compiler_params=_params(("parallel",)),
    )(z, z, norm_g, norm_b, spatial_w, spatial_bt, tail)


def _mix_sample_kernel(val_ref, gate_ref, u_ref, v_ref, st_ref, cw_ref, cb_ref, ag_ref, ab_ref,
                       bg_ref, bb_ref, sw_ref, sb_ref, y_ref, ain_ref, vout_ref, *, kw, groups, width):
    n = val_ref.shape[0]
    y_ref[...] = jnp.zeros(y_ref.shape, y_ref.dtype)
    a_in = val_ref[...] * jax.nn.sigmoid(gate_ref[...])
    ain_ref[...] = a_in
    acc = cw_ref[0:1, :] * st_ref[0]
    for k in range(1, kw - 1):
        acc = acc + cw_ref[k:k + 1, :] * st_ref[k]
    acc = acc + cw_ref[kw - 1:kw, :] * a_in + cb_ref[...]
    for g in range(groups):
        cols = slice(g * LANES, (g + 1) * LANES)
        y = _layer_norm(acc[:, cols], ag_ref[:, cols], ab_ref[:, cols])
        y_ref[0:n, cols] = _silu(y)
    v = _layer_norm(_gelu(v_ref[...]), bg_ref[...], bb_ref[...])
    vout_ref[...] = v
    y_ref[0:n, width:2 * width] = _gelu(u_ref[...]) * (sw_ref[...] * v + sb_ref[...])


def _mix_sample(z, state_t, conv_w, conv_b, a_g, a_b, b_g, b_b, s_w0, s_b0, layer, *, row0, n, width):
    tail = z.shape[0] - row0
    kw = conv_w.shape[1]
    rb = row0 // n
    zspec = lambda c: pl.BlockSpec((n, width), lambda i: (rb, c))
    vec = lambda: pl.BlockSpec((None, 1, width), lambda i: (layer, 0, 0))
    return pl.pallas_call(
        functools.partial(_mix_sample_kernel, kw=kw, groups=A_GROUPS, width=width),
        grid=(1,),
        in_specs=[
            zspec(0), zspec(1), zspec(2), zspec(3),
            pl.BlockSpec((kw - 1, n, width), lambda i: (0, 0, 0)),
            pl.BlockSpec((None, kw, width), lambda i: (layer, 0, 0)),
            vec(), vec(), vec(), vec(), vec(), vec(), vec(),
        ],
        out_specs=[
            pl.BlockSpec((tail, 2 * width), lambda i: (0, 0)),
            pl.BlockSpec((n, width), lambda i: (0, 0)),
            pl.BlockSpec((n, width), lambda i: (0, 0)),
        ],
        out_shape=[
            jax.ShapeDtypeStruct((tail, 2 * width), F32),
            jax.ShapeDtypeStruct((n, width), F32),
            jax.ShapeDtypeStruct((n, width), F32),
        ],
        compiler_params=_params(("arbitrary",)),
    )(z, z, z, z, state_t, conv_w, conv_b, a_g, a_b, b_g, b_b, s_w0, s_b0)


def _rows(start, size, stride):
    return pl.ds(start, size) if stride == 1 else pl.ds(start, size, stride=stride)


LOG2E = 1.4426950408889634


def _attn_prompt_kernel(q_ref, k_ref, v_ref, tail_ref, o_ref, bias1_ref, bias2_ref, *stats,
                        batch, patterns, blk, scale):
    @pl.when(pl.program_id(0) == batch)
    def _():
        o_ref[0:tail_ref.shape[0], :] = tail_ref[...].astype(o_ref.dtype)

    @pl.when(pl.program_id(0) < batch)
    def _():
        _attn_prompt_sequence(q_ref, k_ref, v_ref, o_ref, bias1_ref, bias2_ref, stats, patterns, blk, scale)


def _attn_prompt_sequence(q_ref, k_ref, v_ref, o_ref, bias1_ref, bias2_ref, stats, patterns, blk, scale):
    seq, e = q_ref.shape
    npat = len(patterns)
    acc_refs, m_refs, l_refs = stats[:npat], stats[npat:2 * npat], stats[2 * npat:]
    c = scale * LOG2E

    def distance(nk):
        qi = lax.broadcasted_iota(jnp.int32, (blk, nk), 0)
        kj = lax.broadcasted_iota(jnp.int32, (blk, nk), 1)
        return (nk - blk) + qi - kj

    def attend(q_rows, k_rows, bias):
        qb = q_ref[q_rows, :].astype(BF16)
        kb = k_ref[k_rows, :].astype(BF16)
        vb = v_ref[k_rows, :].astype(BF16)
        s = lax.dot_general(qb, kb, (((1,), (1,)), ((), ())), preferred_element_type=F32) + bias
        m = jnp.max(s, axis=-1, keepdims=True)
        p = jnp.exp2((s - m) * c).astype(BF16)
        ov = jnp.dot(p, jnp.concatenate([vb, jnp.ones_like(vb)], axis=1), preferred_element_type=F32)
        return ov[:, :e], ov[:, e:], jnp.broadcast_to(m, (blk, e))

    for p, (window, dil) in enumerate(patterns):
        reach = window // dil
        nblk = seq // dil // blk
        for bias_ref in (bias1_ref, bias2_ref):
            dist = distance(bias_ref.shape[1])
            bias_ref[...] = jnp.where((dist >= 0) & (dist <= reach), 0.0, -jnp.inf)
        for r in range(dil):
            for b in range(nblk):
                q_rows = _rows(r + dil * blk * b, blk, dil)
                if b == 0:
                    k_rows, bias = q_rows, bias1_ref[...]
                else:
                    k_rows, bias = _rows(r + dil * blk * (b - 1), 2 * blk, dil), bias2_ref[...]
                acc_refs[p][q_rows, :], l_refs[p][q_rows, :], m_refs[p][q_rows, :] = attend(q_rows, k_rows, bias)

    def merge(t, carry):
        rows = pl.ds(pl.multiple_of(t * blk, blk), blk)
        ms = [m_ref[rows, :] for m_ref in m_refs]
        top = functools.reduce(jnp.maximum, ms)
        ws = [jnp.exp2((m - top) * c) for m in ms]
        den = functools.reduce(lambda x, y: x + y, [w * l_ref[rows, :] for w, l_ref in zip(ws, l_refs)])
        num = functools.reduce(lambda x, y: x + y, [w * a_ref[rows, :] for w, a_ref in zip(ws, acc_refs)])
        o_ref[rows, :] = (num / den).astype(o_ref.dtype)
        return carry

    lax.fori_loop(0, seq // blk, merge, 0, unroll=2)


def _attn_prompt(qkv, tail, *, batch, seq, heads):
    e = qkv.shape[1] // (3 * heads)
    assert e == LANES and tail.shape[0] == qkv.shape[0] - batch * seq and tail.shape[0] <= seq
    for window, dil in C_PATTERNS:
        assert window // dil <= C_BLOCK and (seq // dil) % C_BLOCK == 0
    spec = pl.BlockSpec((seq, e), lambda b, h: (b, h))
    part = lambda c: pl.BlockSpec((seq, e), lambda b, h: (b, c * heads + h))
    nstats = 3 * len(C_PATTERNS)
    return pl.pallas_call(
        functools.partial(_attn_prompt_kernel, batch=batch, patterns=C_PATTERNS, blk=C_BLOCK, scale=e ** -0.5),
        grid=(batch + 1, heads),
        in_specs=[part(0), part(1), part(2), pl.BlockSpec((tail.shape[0], e), lambda b, h: (0, h))],
        out_specs=spec,
        out_shape=jax.ShapeDtypeStruct((qkv.shape[0], heads * e), BF16),
        scratch_shapes=(
            [pltpu.VMEM((C_BLOCK, C_BLOCK), F32), pltpu.VMEM((C_BLOCK, 2 * C_BLOCK), F32)]
            + [pltpu.VMEM((seq, e), F32) for _ in range(nstats)]
        ),
        compiler_params=_params(("parallel", "parallel")),
    )(qkv, qkv, qkv, tail)


def _attn_sample_kernel(q_ref, kn_ref, vn_ref, *refs, npat, scale):
    kc_refs, vc_refs, o_ref = refs[:npat], refs[npat:2 * npat], refs[2 * npat]
    q = q_ref[...]
    s_new = jnp.sum(q * kn_ref[...], axis=-1, keepdims=True) * scale
    scores = [jnp.sum(kc[...] * q[None], axis=-1, keepdims=True) * scale for kc in kc_refs]
    top = s_new
    for s in scores:
        top = jnp.maximum(top, jnp.max(s, axis=0))
    w_new = npat * jnp.exp(s_new - top)
    den = w_new
    num = w_new * vn_ref[...]
    for s, vc in zip(scores, vc_refs):
        p = jnp.exp(s - top[None])
        den = den + jnp.sum(p, axis=0)
        num = num + jnp.sum(p * vc[...], axis=0)
    o_ref[...] = num / den


def _attn_sample(q_s, k_s, v_s, cache_k, cache_v, layer, *, head_group=32):
    n, heads, e = q_s.shape
    hist = cache_k.shape[2]
    new = pl.BlockSpec((None, head_group, e), lambda i, g: (i, g, 0))
    views, specs = [], []
    for window, dil in C_PATTERNS:
        reach = window // dil
        assert window <= hist and hist % dil == 0 and (hist // dil) % reach == 0
        last = hist // dil // reach - 1
        views.append((cache_k.shape[0], n, hist // dil, dil, heads, e))
        specs.append(pl.BlockSpec((None, None, reach, None, head_group, e),
                                  lambda i, g, last=last: (layer, i, last, 0, g, 0)))
    return pl.pallas_call(
        functools.partial(_attn_sample_kernel, npat=len(C_PATTERNS), scale=e ** -0.5),
        grid=(n, heads // head_group),
        in_specs=[new, new, new] + specs + specs,
        out_specs=new,
        out_shape=jax.ShapeDtypeStruct((n, heads, e), F32),
        compiler_params=_params(("parallel", "parallel")),
    )(q_s, k_s, v_s, *[cache_k.reshape(v) for v in views], *[cache_v.reshape(v) for v in views])


def _cache_set_last_kernel(new_ref, shifted_ref, o_ref):
    del shifted_ref
    o_ref[...] = new_ref[...]


def _cache_set_last(shifted, new):
    n, hist, heads, e = shifted.shape
    return pl.pallas_call(
        _cache_set_last_kernel,
        grid=(n,),
        in_specs=[pl.BlockSpec((None, 1, heads, e), lambda i: (i, 0, 0, 0)), pl.BlockSpec(memory_space=pl.ANY)],
        out_specs=pl.BlockSpec((None, 1, heads, e), lambda i: (i, hist - 1, 0, 0)),
        out_shape=jax.ShapeDtypeStruct(shifted.shape, shifted.dtype),
        input_output_aliases={1: 0},
        compiler_params=_params(("parallel",)),
    )(new, shifted)


def kernel(x_prompt, x_sample, state_a_conv, cache_c_k, cache_c_v, ln_mix_even, w_in_even, a_conv_w,
           a_conv_b, a_norm_g, a_norm_b, b_norm_g, b_norm_b, b_spatial_w, b_spatial_b, w_out_even,
           ln_mix_odd, w_qkv_odd, w_o_odd, ln_ffn, w_gate_up, w_down, ln_final):
    batch, seq, d = x_prompt.shape
    nsamp = x_sample.shape[0] * x_sample.shape[1]
    depth = ln_ffn.shape[0]
    np_rows = batch * seq
    rows_all = pl.cdiv(np_rows + nsamp, ROW_TILE) * ROW_TILE
    tail = rows_all - np_rows
    assert rows_all % NORM_ROW_TILE == 0 and np_rows % RMS_TILE_OUT == 0 and np_rows % nsamp == 0
    width = a_conv_w.shape[-1]
    ffn = w_down.shape[1]
    e = d // C_HEADS

    def tail_rows(s):
        return jnp.concatenate([s, jnp.zeros((tail - nsamp, s.shape[1]), s.dtype)], axis=0)

    vec3 = lambda t: t.reshape(t.shape[0], 1, t.shape[-1])
    ln_mix = (vec3(ln_mix_even), vec3(ln_mix_odd))

    def mix_gain(i):
        return ln_mix[i % 2], i // 2

    x, xb, rstd = _stack_rows(x_prompt.reshape(np_rows, d), tail_rows(x_sample.reshape(nsamp, d)), *mix_gain(0))

    a_hist_p, a_hist_s, b_v_s = [], [], []
    c_k_p, c_v_p = [], []
    rides = [(c, j) for j in range(cache_c_k.shape[0]) for c in (cache_c_k, cache_c_v)]
    assert len(rides) <= depth
    shifted, new_rows = [], []
    for i in range(depth):
        j = i // 2
        if i % 2 == 0:
            z = _matmul([xb], w_in_even, j, n=w_in_even.shape[-1], tm=ROW_TILE, rstd=rstd)
            s_w0 = jnp.repeat(b_spatial_w[:, :, 0, 0], width // B_GROUPS, axis=-1)
            s_b0 = jnp.repeat(b_spatial_b[:, :, 0], width // B_GROUPS, axis=-1)
            y_s, a_in_s, v_s = _mix_sample(
                z, jnp.swapaxes(state_a_conv[j], 0, 1), a_conv_w, vec3(a_conv_b), vec3(a_norm_g),
                vec3(a_norm_b), vec3(b_norm_g), vec3(b_norm_b), vec3(s_w0), vec3(s_b0), j,
                row0=np_rows, n=nsamp, width=width)
            a, hist_p = _mixa_prompt(z, y_s, a_conv_w, vec3(a_conv_b), vec3(a_norm_g), vec3(a_norm_b), j,
                                     batch=batch, seq=seq, width=width)
            b = _mixb_prompt(z, y_s, vec3(b_norm_g), vec3(b_norm_b), b_spatial_w,
                             jnp.swapaxes(b_spatial_b, 1, 2), j, rows=np_rows, width=width)
            x, xb, rstd = _matmul([a, b], w_out_even, j, n=d, tm=NORM_ROW_TILE, res=x, next_gain=(vec3(ln_ffn), i))
            a_hist_p.append(hist_p)
            a_hist_s.append(jnp.concatenate([state_a_conv[j][:, 1:], a_in_s[:, None, :]], axis=1))
            b_v_s.append(v_s.reshape(x_sample.shape[0], x_sample.shape[1], width))
        else:
            qkv = _matmul([xb], w_qkv_odd, j, n=3 * d, tm=ROW_TILE, rstd=rstd)
            qkv_s = qkv[np_rows:np_rows + nsamp].reshape(nsamp, 3, C_HEADS, e)
            o_s = _attn_sample(qkv_s[:, 0], qkv_s[:, 1], qkv_s[:, 2], cache_c_k, cache_c_v, j)
            o = _attn_prompt(qkv, tail_rows(o_s.reshape(nsamp, d)), batch=batch, seq=seq, heads=C_HEADS)
            x, xb, rstd = _matmul([o], w_o_odd, j, n=d, tm=NORM_ROW_TILE, res=x, next_gain=(vec3(ln_ffn), i))
            keep = min(C_PATTERNS[-1][0], seq)
            c_k_p.append(qkv[:np_rows, d:2 * d].reshape(batch, seq, C_HEADS, e)[:, seq - keep:])
            c_v_p.append(qkv[:np_rows, 2 * d:].reshape(batch, seq, C_HEADS, e)[:, seq - keep:])
            new_rows += [qkv_s[:, 1:2], qkv_s[:, 2:3]]
        hid, w_down_bf16, bulk = _matmul_gated(xb, w_gate_up, i, rstd, (w_down, i), half=ffn,
                                               ride=rides[i] if i < len(rides) else None)
        shifted.append(bulk)
        down = functools.partial(_matmul, [hid], w_down_bf16[None], 0, n=d, tm=DOWN_ROW_TILE, nk=2, res=x)
        if i + 1 < depth:
            x, xb, rstd = down(next_gain=mix_gain(i + 1))
        else:
            x = down()

    c_s = [_cache_set_last(bulk, new) for bulk, new in zip(shifted, new_rows)]
    c_k_s, c_v_s = c_s[0::2], c_s[1::2]
    g_final = ln_final.reshape(1, 1, d)
    y_prompt = _rmsnorm(x, g_final, 0, F32, rows=np_rows, row_tile=RMS_TILE_OUT)
    y_sample = _rmsnorm(x, g_final, 0, F32, rows=nsamp, row_tile=nsamp, first_block=np_rows // nsamp)
    return (y_prompt.reshape(batch, seq, d), y_sample.reshape(x_sample.shape),
            jnp.stack(a_hist_p), jnp.stack(a_hist_s), jnp.stack(b_v_s),
            jnp.stack(c_k_p), jnp.stack(c_v_p), jnp.stack(c_k_s), jnp.stack(c_v_s))
```
